```python
import math
import jax
import jax.numpy as jnp
from jax import lax
import numpy as np

D_MODEL = 2048
BATCH = 16
SEQ = 256
DEPTH = 4
DEC_BATCH = 4
DEC_SEQ = 4096
PAST_LEN = 512

GRID_W = 64
D_FF = 5632
N_MOD = 9
NORM_EPS = 1e-6
D_A = D_MODEL // 2
A_GROUP = 16
G_A = D_A // A_GROUP
P_A = 64
DT_MIN = 1e-3
DT_MAX = 1e-1
D_B = D_MODEL - D_A
N_HEADS_B = 8
DH_B = D_B // N_HEADS_B
WIN_R = 8
WIN_C = 16
D_C = D_MODEL
CONV_W = 3
N_EVEN = (DEPTH + 1) // 2
N_ODD = DEPTH // 2
NEG_INF = -1e30

kernel_name = 'hybrid_s5_natten_shortconv_prefix_dit_step'


def rms_norm(x, g):
    xf = x.astype(jnp.float32)
    y = xf * lax.rsqrt(jnp.mean(xf * xf, axis=-1, keepdims=True) + NORM_EPS)
    return (y * g.astype(jnp.float32)).astype(x.dtype)


def modulation(cvec, w_mod, b_mod):
    m = jax.nn.silu(cvec) @ w_mod + b_mod
    return m.reshape(m.shape[:-1] + (N_MOD, D_MODEL))


def modulate(x, g_pre, shift, scale):
    return rms_norm(x, g_pre) * (1.0 + scale) + shift


def residual(x, o, g_post, gate, res_w):
    return x + res_w * gate * rms_norm(o, g_post)


def swiglu(h, w_in, w_out):
    g, u = jnp.split(h @ w_in, 2, axis=-1)
    return (jax.nn.silu(g) * u) @ w_out


def half_ffn(x, shift, scale, gate, g_pre, g_post, w_in, w_out):
    return residual(x, swiglu(modulate(x, g_pre, shift, scale), w_in, w_out), g_post, gate, 0.5)


def _linear_combine(e1, e2):
    a1, b1 = e1
    a2, b2 = e2
    return a1 * a2, a2 * b1 + b2


def s5_bidir(u, s0_re, s0_im, lam_re, lam_im, log_dt, b_re, b_im, c_re, c_im, d_skip, w_glu, keep_final):
    f32 = jnp.float32
    bn, L, _ = u.shape
    ug = u.astype(f32).reshape(bn, L, G_A, A_GROUP)
    y = ug * d_skip.astype(f32)
    finals = []
    for di in range(2):
        lam = lax.complex(lam_re[di].astype(f32), lam_im[di].astype(f32))
        dt = jnp.exp(log_dt[di].astype(f32))[:, None]
        lam_bar = jnp.exp(lam * dt)
        b = lax.complex(b_re[di].astype(f32), b_im[di].astype(f32))
        b_bar = ((lam_bar - 1.0) / lam)[..., None] * b
        bu = lax.complex(jnp.einsum('blgh,gph->blgp', ug, jnp.real(b_bar)),
                         jnp.einsum('blgh,gph->blgp', ug, jnp.imag(b_bar)))
        a = jnp.broadcast_to(lam_bar, bu.shape)
        a_cum, h = lax.associative_scan(_linear_combine, (a, bu), reverse=(di == 1), axis=1)
        if s0_re is not None:
            s0 = lax.complex(s0_re[:, di].astype(f32), s0_im[:, di].astype(f32))
            h = h + a_cum * s0[:, None]
        if keep_final:
            finals.append(h[:, -1] if di == 0 else h[:, 0])
        y = (y + jnp.einsum('blgp,ghp->blgh', jnp.real(h), c_re[di].astype(f32))
               - jnp.einsum('blgp,ghp->blgh', jnp.imag(h), c_im[di].astype(f32)))
    y = jax.nn.gelu(y.reshape(bn, L, D_A))
    y = (y * jax.nn.sigmoid(y @ w_glu.astype(f32))).astype(u.dtype)
    if keep_final:
        fin = jnp.stack(finals, axis=1)
        return y, jnp.real(fin), jnp.imag(fin)
    return y, None, None


def split_even(z):
    bn, L, _ = z.shape
    u = z[..., :D_A]
    q, k, v = jnp.split(z[..., D_A:], 3, axis=-1)
    shp = (bn, L, N_HEADS_B, DH_B)
    return u, q.reshape(shp), k.reshape(shp), v.reshape(shp)


def context_attention(q, k, v):
    f32 = jnp.float32
    bn, L, H, dh = q.shape
    s = jnp.einsum('bqhd,bkhd->bhqk', q.astype(f32), k.astype(f32)) * (dh ** -0.5)
    p = jax.nn.softmax(s, axis=-1)
    o = jnp.einsum('bhqk,bkhd->bqhd', p, v.astype(f32))
    return o.reshape(bn, L, H * dh).astype(q.dtype)


def neighbourhood_attention(q, k, v, k_ctx, v_ctx, rpb):
    f32 = jnp.float32
    bn, L, H, dh = q.shape
    rows = L // GRID_W
    wr = min(WIN_R, rows)
    qg = q.astype(f32).reshape(bn, rows, GRID_W, H, dh)
    kg = k.astype(f32).reshape(bn, rows, GRID_W, H, dh)
    vg = v.astype(f32).reshape(bn, rows, GRID_W, H, dh)
    r = jnp.arange(rows)
    key_rows = jnp.clip(r - wr // 2, 0, rows - wr)[:, None] + jnp.arange(wr)[None, :]
    nk = wr * GRID_W
    kb = kg[:, key_rows].reshape(bn, rows, nk, H, dh)
    vb = vg[:, key_rows].reshape(bn, rows, nk, H, dh)
    col = jnp.arange(GRID_W)
    c0 = jnp.clip(col - WIN_C // 2, 0, GRID_W - WIN_C)
    col_ok = (col[None, :] >= c0[:, None]) & (col[None, :] < c0[:, None] + WIN_C)
    ir = key_rows - r[:, None] + (WIN_R - 1)
    ic = jnp.clip(col[None, :] - col[:, None] + (WIN_C - 1), 0, 2 * WIN_C - 2)
    bias = rpb.astype(f32)[:, ir[:, None, :, None], ic[None, :, None, :]]
    bias = jnp.where(col_ok[None, None, :, None, :], bias, NEG_INF)
    bias = jnp.transpose(bias, (1, 0, 2, 3, 4)).reshape(rows, H, GRID_W, nk)
    scale = dh ** -0.5
    s_loc = jnp.einsum('brqhd,brkhd->brhqk', qg, kb) * scale + bias
    s_ctx = jnp.einsum('brqhd,bchd->brhqc', qg, k_ctx.astype(f32)) * scale
    p = jax.nn.softmax(jnp.concatenate([s_loc, s_ctx], axis=-1), axis=-1)
    o = (jnp.einsum('brhqk,brkhd->brqhd', p[..., :nk], vb)
         + jnp.einsum('brhqc,bchd->brqhd', p[..., nk:], v_ctx.astype(f32)))
    return o.reshape(bn, L, H * dh).astype(q.dtype)


def short_conv(x, w):
    return lax.conv_general_dilated(x, w[:, None, :].astype(x.dtype), window_strides=(1,),
                                    padding=[(CONV_W // 2, CONV_W // 2)],
                                    dimension_numbers=('NWC', 'WIO', 'NWC'),
                                    feature_group_count=x.shape[-1])


def gated_short_conv(h, w_in, conv_w, w_out):
    b, cg, xt = jnp.split(h @ w_in, 3, axis=-1)
    return (b * short_conv(cg * xt, conv_w)) @ w_out


def setup_inputs(seed: int = 0) -> dict:
    key = jax.random.key(seed)
    ks = jax.random.split(key, 28)
    f32 = jnp.float32

    def nrm(k, shape, s):
        return jax.random.normal(k, shape, f32) * s

    d = D_MODEL
    p_idx = jnp.arange(P_A, dtype=f32)
    s5_shape = (N_EVEN, 2, G_A, P_A)
    return {
        'x_prompt': nrm(ks[0], (BATCH, SEQ, d), 1.0),
        'x_sample': nrm(ks[1], (DEC_BATCH, DEC_SEQ, d), 1.0),
        'cache_k': nrm(ks[2], (DEC_BATCH, N_EVEN, PAST_LEN, N_HEADS_B, DH_B), 1.0),
        'cache_v': nrm(ks[3], (DEC_BATCH, N_EVEN, PAST_LEN, N_HEADS_B, DH_B), 1.0),
        'state_s5_re': nrm(ks[4], (DEC_BATCH, N_EVEN, 2, G_A, P_A), 0.5),
        'state_s5_im': nrm(ks[5], (DEC_BATCH, N_EVEN, 2, G_A, P_A), 0.5),
        'c': nrm(ks[6], (DEC_BATCH, d), 1.0),
        'c_ctx': nrm(ks[7], (d,), 1.0),
        'norm_g': 1.0 + nrm(ks[8], (DEPTH, 6, d), 0.02),
        'w_mod': nrm(ks[9], (DEPTH, d, N_MOD * d), 0.5 * d ** -0.5),
        'b_mod': nrm(ks[10], (DEPTH, N_MOD * d), 0.01),
        'w_ffn_in': nrm(ks[11], (DEPTH, 2, d, 2 * D_FF), d ** -0.5),
        'w_ffn_out': nrm(ks[12], (DEPTH, 2, D_FF, d), D_FF ** -0.5),
        'w_in_even': nrm(ks[13], (N_EVEN, d, D_A + 3 * D_B), d ** -0.5),
        'w_out_even': nrm(ks[14], (N_EVEN, D_A + D_B, d), (D_A + D_B) ** -0.5),
        's5_lam_re': -0.5 + nrm(ks[15], s5_shape, 0.01),
        's5_lam_im': math.pi * p_idx + nrm(ks[16], s5_shape, 0.01),
        's5_log_dt': jax.random.uniform(ks[17], (N_EVEN, 2, G_A), f32, math.log(DT_MIN), math.log(DT_MAX)),
        's5_b_re': nrm(ks[18], (N_EVEN, 2, G_A, P_A, A_GROUP), (2 * A_GROUP) ** -0.5),
        's5_b_im': nrm(ks[19], (N_EVEN, 2, G_A, P_A, A_GROUP), (2 * A_GROUP) ** -0.5),
        's5_c_re': nrm(ks[20], (N_EVEN, 2, G_A, A_GROUP, P_A), (2 * P_A) ** -0.5),
        's5_c_im': nrm(ks[21], (N_EVEN, 2, G_A, A_GROUP, P_A), (2 * P_A) ** -0.5),
        's5_d': nrm(ks[22], (N_EVEN, G_A, A_GROUP), 1.0),
        'w_glu': nrm(ks[23], (N_EVEN, D_A, D_A), D_A ** -0.5),
        'na_rpb': nrm(ks[24], (N_EVEN, N_HEADS_B, 2 * WIN_R - 1, 2 * WIN_C - 1), 0.02),
        'w_in_conv': nrm(ks[25], (N_ODD, d, 3 * D_C), d ** -0.5),
        'conv_w': nrm(ks[26], (N_ODD, CONV_W, D_C), CONV_W ** -0.5),
        'w_out_conv': nrm(ks[27], (N_ODD, D_C, d), D_C ** -0.5),
    }


def reference(x_prompt, x_sample, cache_k, cache_v, state_s5_re, state_s5_im, c, c_ctx,
              norm_g, w_mod, b_mod, w_ffn_in, w_ffn_out, w_in_even, w_out_even,
              s5_lam_re, s5_lam_im, s5_log_dt, s5_b_re, s5_b_im, s5_c_re, s5_c_im, s5_d, w_glu,
              na_rpb, w_in_conv, conv_w, w_out_conv):
    xc = x_prompt
    xl = x_sample
    new_k, new_v, new_sre, new_sim = [], [], [], []
    for l in range(DEPTH):
        g = norm_g[l]
        m_ctx = modulation(c_ctx, w_mod[l], b_mod[l])
        mc = [m_ctx[i] for i in range(N_MOD)]
        m_lat = modulation(c, w_mod[l], b_mod[l])
        ml = [m_lat[:, i, None, :] for i in range(N_MOD)]
        xc = half_ffn(xc, mc[0], mc[1], mc[2], g[0], g[1], w_ffn_in[l, 0], w_ffn_out[l, 0])
        xl = half_ffn(xl, ml[0], ml[1], ml[2], g[0], g[1], w_ffn_in[l, 0], w_ffn_out[l, 0])
        hc = modulate(xc, g[2], mc[3], mc[4])
        hl = modulate(xl, g[2], ml[3], ml[4])
        if l % 2 == 0:
            e = l // 2
            s5p = (s5_lam_re[e], s5_lam_im[e], s5_log_dt[e], s5_b_re[e], s5_b_im[e],
                   s5_c_re[e], s5_c_im[e], s5_d[e], w_glu[e])
            uc, qc, kc, vc = split_even(hc @ w_in_even[e])
            ya_c, fin_re, fin_im = s5_bidir(uc, None, None, *s5p, keep_final=True)
            yb_c = context_attention(qc, kc, vc)
            oc = jnp.concatenate([ya_c, yb_c], axis=-1) @ w_out_even[e]
            new_k.append(kc)
            new_v.append(vc)
            new_sre.append(fin_re)
            new_sim.append(fin_im)
            ul, ql, kl, vl = split_even(hl @ w_in_even[e])
            ya_l, _, _ = s5_bidir(ul, state_s5_re[:, e], state_s5_im[:, e], *s5p, keep_final=False)
            yb_l = neighbourhood_attention(ql, kl, vl, cache_k[:, e], cache_v[:, e], na_rpb[e])
            ol = jnp.concatenate([ya_l, yb_l], axis=-1) @ w_out_even[e]
        else:
            o = l // 2
            oc = gated_short_conv(hc, w_in_conv[o], conv_w[o], w_out_conv[o])
            ol = gated_short_conv(hl, w_in_conv[o], conv_w[o], w_out_conv[o])
        xc = residual(xc, oc, g[3], mc[5], 1.0)
        xl = residual(xl, ol, g[3], ml[5], 1.0)
        xc = half_ffn(xc, mc[6], mc[7], mc[8], g[4], g[5], w_ffn_in[l, 1], w_ffn_out[l, 1])
        xl = half_ffn(xl, ml[6], ml[7], ml[8], g[4], g[5], w_ffn_in[l, 1], w_ffn_out[l, 1])
    y_prompt = xc
    y_sample = xl
    new_cache_k = jnp.stack(new_k, axis=1)
    new_cache_v = jnp.stack(new_v, axis=1)
    new_state_s5_re = jnp.stack(new_sre, axis=1)
    new_state_s5_im = jnp.stack(new_sim, axis=1)
    return (y_prompt, y_sample, new_cache_k, new_cache_v, new_state_s5_re, new_state_s5_im)
```

```python
import functools
import math

import numpy as np
import jax
import jax.numpy as jnp
from jax import lax
from jax.experimental import pallas as pl
from jax.experimental.pallas import tpu as pltpu

F32 = jnp.float32
BF16 = jnp.bfloat16

NORM_EPS = 1e-6
N_MOD = 9
GRID_W = 64
NEG_INF = -1e30
S5_CHUNK = 16
NA_QROWS = 4
LANE = 128
SUBLANE = 8
VMEM_LIMIT_BYTES = 56 * 1024 * 1024
HI = lax.Precision.HIGHEST


def _cparams(sem):
    return pltpu.CompilerParams(dimension_semantics=sem, vmem_limit_bytes=VMEM_LIMIT_BYTES)


def _tile(n, pref):
    t = min(n, pref)
    while n % t:
        t //= 2
    return t


def _resident(block_shape, index_map):
    return pl.BlockSpec(block_shape, index_map, pipeline_mode=pl.Buffered(1))


def _rms(x):
    return x * lax.rsqrt(jnp.mean(x * x, axis=-1, keepdims=True) + NORM_EPS)


def _modnorm(x, g, shift, scale):
    return (_rms(x) * g) * (1.0 + scale) + shift


def _dot(a, b):
    return jnp.dot(a, b, preferred_element_type=F32)


def _dot_nt(a, b):
    return lax.dot_general(a, b, (((1,), (1,)), ((), ())), preferred_element_type=F32)


def _mod_kernel(c_ref, w_ref, b_ref, o_ref):
    c = c_ref[...]
    s = (c * jax.nn.sigmoid(c)).astype(BF16)
    o_ref[0] = _dot(s, w_ref[0].astype(BF16)) + b_ref[0]


def _modulation(cvec, w_mod, b_mod):
    depth, d, nd = w_mod.shape
    ng = cvec.shape[0]
    tn = _tile(nd, 1024)
    return pl.pallas_call(
        _mod_kernel,
        grid=(depth, nd // tn),
        in_specs=[
            pl.BlockSpec((ng, d), lambda l, j: (0, 0)),
            pl.BlockSpec((1, d, tn), lambda l, j: (l, 0, j)),
            pl.BlockSpec((1, 1, tn), lambda l, j: (l, 0, j)),
        ],
        out_specs=pl.BlockSpec((1, ng, tn), lambda l, j: (l, 0, j)),
        out_shape=jax.ShapeDtypeStruct((depth, ng, nd), F32),
        name="modulation",
        compiler_params=_cparams(("arbitrary", "arbitrary")),
    )(cvec, w_mod, b_mod.reshape(depth, 1, nd))


def _ffn_kernel(x_ref, m_ref, g_ref, wg_ref, wu_ref, wo_ref, o_ref, h_ref, *, mi, gi, nj):
    j = pl.program_id(1)

    @pl.when(j == 0)
    def _():
        h_ref[...] = _modnorm(x_ref[...], g_ref[gi:gi + 1, :], m_ref[0, mi:mi + 1, :],
                              m_ref[0, mi + 1:mi + 2, :]).astype(BF16)

    h = h_ref[...]
    a = _dot(h, wg_ref[...])
    u = _dot(h, wu_ref[...])
    hid = (a * jax.nn.sigmoid(a) * u).astype(BF16)
    part = _dot(hid, wo_ref[...])

    @pl.when(j == 0)
    def _():
        o_ref[...] = part

    @pl.when(j > 0)
    def _():
        o_ref[...] += part

    @pl.when(j == nj - 1)
    def _():
        r = _rms(o_ref[...]) * g_ref[gi + 1:gi + 2, :]
        o_ref[...] = x_ref[...] + 0.5 * m_ref[0, mi + 2:mi + 3, :] * r


def _ffn(x, mod_l, g_l, w_in, w_out, *, mi, gi, gid, tm):
    n, d = x.shape
    dff = w_out.shape[0]
    tf = _tile(dff, 512)
    nj = dff // tf
    kern = functools.partial(_ffn_kernel, mi=mi, gi=gi, nj=nj)
    return pl.pallas_call(
        kern,
        grid=(n // tm, nj),
        in_specs=[
            pl.BlockSpec((tm, d), lambda i, j: (i, 0)),
            pl.BlockSpec((1, N_MOD, d), lambda i, j: (gid(i), 0, 0)),
            pl.BlockSpec(g_l.shape, lambda i, j: (0, 0)),
            pl.BlockSpec((d, tf), lambda i, j: (0, j)),
            pl.BlockSpec((d, tf), lambda i, j: (0, j + nj)),
            pl.BlockSpec((tf, d), lambda i, j: (j, 0)),
        ],
        out_specs=pl.BlockSpec((tm, d), lambda i, j: (i, 0)),
        out_shape=jax.ShapeDtypeStruct((n, d), F32),
        scratch_shapes=[pltpu.VMEM((tm, d), BF16)],
        name="ffn",
        compiler_params=_cparams(("arbitrary", "arbitrary")),
    )(x, mod_l, g_l, w_in, w_in, w_out)


def _even_in_kernel(x_ref, m_ref, g_ref, w_ref, u_ref, q_ref, k_ref, v_ref, h_ref):
    j = pl.program_id(1)

    @pl.when(j == 0)
    def _():
        h_ref[...] = _modnorm(x_ref[...], g_ref[2:3, :], m_ref[0, 3:4, :], m_ref[0, 4:5, :]).astype(BF16)

    z = _dot(h_ref[...], w_ref[...])
    for idx, ref in enumerate((u_ref, q_ref, k_ref, v_ref)):
        @pl.when(j == idx)
        def _(ref=ref):
            ref[...] = z.astype(ref.dtype)


def _even_in(x, mod_l, g_l, w, *, gid, tm):
    n, d = x.shape
    dh = w.shape[1] // 4
    blk = pl.BlockSpec((tm, dh), lambda i, j: (i, 0))
    return pl.pallas_call(
        _even_in_kernel,
        grid=(n // tm, 4),
        in_specs=[
            pl.BlockSpec((tm, d), lambda i, j: (i, 0)),
            pl.BlockSpec((1, N_MOD, d), lambda i, j: (gid(i), 0, 0)),
            pl.BlockSpec(g_l.shape, lambda i, j: (0, 0)),
            pl.BlockSpec((d, dh), lambda i, j: (0, j)),
        ],
        out_specs=[blk, blk, blk, blk],
        out_shape=[jax.ShapeDtypeStruct((n, dh), BF16), jax.ShapeDtypeStruct((n, dh), BF16),
                   jax.ShapeDtypeStruct((n, dh), F32), jax.ShapeDtypeStruct((n, dh), F32)],
        scratch_shapes=[pltpu.VMEM((tm, d), BF16)],
        name="even_in",
        compiler_params=_cparams(("arbitrary", "arbitrary")),
    )(x, mod_l, g_l, w)


def _s5_kernel(u_ref, kst_ref, kfull_ref, kout_ref, ar_ref, ai_ref, h0_ref, y_ref, hfin_ref, s_ref,
               *, gb, nseq, nchunk):
    rows = u_ref.shape[0]
    r8 = rows // SUBLANE
    w2 = gb * LANE
    w = 2 * w2
    gw = 2 * LANE

    for g in range(gb):
        res = _dot(u_ref[:, g * gw:(g + 1) * gw], kst_ref[g])
        s_ref[:, :, g * LANE:(g + 1) * LANE] = res[:, :LANE].reshape(r8, SUBLANE, LANE)
        s_ref[:, :, w2 + g * LANE:w2 + (g + 1) * LANE] = res[:, LANE:].reshape(r8, SUBLANE, LANE)

    ar = ar_ref[...]
    ai = ai_ref[...]

    def step(hr, hi, sr, si):
        return ar * hr - ai * hi + sr, ar * hi + ai * hr + si

    if nseq % SUBLANE == 0:
        nb = nseq // SUBLANE
        d0 = (lax.broadcasted_iota(jnp.int32, (nb, SUBLANE, w2), 2) & (LANE // 2)) == 0
        hr0 = h0_ref[:, 0:w2].reshape(nb, SUBLANE, w2)
        hi0 = h0_ref[:, w2:w].reshape(nb, SUBLANE, w2)

        def body(k, carry):
            hr, hi = carry
            k1 = nchunk - 1 - k
            b0r = s_ref[pl.ds(k * nb, nb), :, 0:w2]
            b0i = s_ref[pl.ds(k * nb, nb), :, w2:w]
            b1r = s_ref[pl.ds(k1 * nb, nb), :, 0:w2]
            b1i = s_ref[pl.ds(k1 * nb, nb), :, w2:w]
            s_ref[pl.ds(k * nb, nb), :, 0:w2] = jnp.where(d0, hr, b0r)
            s_ref[pl.ds(k * nb, nb), :, w2:w] = jnp.where(d0, hi, b0i)
            s_ref[pl.ds(k1 * nb, nb), :, 0:w2] = jnp.where(d0, b1r, hr)
            s_ref[pl.ds(k1 * nb, nb), :, w2:w] = jnp.where(d0, b1i, hi)
            return step(hr, hi, jnp.where(d0, b0r, b1r), jnp.where(d0, b0i, b1i))

        assert nchunk % 2 == 0
        hr, hi = lax.fori_loop(0, nchunk, body, (hr0, hi0))
        hfin_ref[:, 0:w2] = hr.reshape(nseq, w2)
        hfin_ref[:, w2:w] = hi.reshape(nseq, w2)
    else:
        half = SUBLANE // 2
        assert nseq == half and nchunk % 4 == 0
        nblk = nchunk // 2
        d0 = (lax.broadcasted_iota(jnp.int32, (half, w2), 1) & (LANE // 2)) == 0
        lo = slice(0, half)
        up = slice(half, SUBLANE)

        def body(k, carry):
            hr, hi = carry
            k1 = nblk - 1 - k
            b0r = s_ref[k, :, 0:w2]
            b0i = s_ref[k, :, w2:w]
            b1r = s_ref[k1, :, 0:w2]
            b1i = s_ref[k1, :, w2:w]
            for fa, ba in ((lo, up), (up, lo)):
                s_ref[k, fa, 0:w2] = jnp.where(d0, hr, b0r[fa])
                s_ref[k, fa, w2:w] = jnp.where(d0, hi, b0i[fa])
                s_ref[k1, ba, 0:w2] = jnp.where(d0, b1r[ba], hr)
                s_ref[k1, ba, w2:w] = jnp.where(d0, b1i[ba], hi)
                hr, hi = step(hr, hi, jnp.where(d0, b0r[fa], b1r[ba]), jnp.where(d0, b0i[fa], b1i[ba]))
            return hr, hi

        hr, hi = lax.fori_loop(0, nblk, body, (h0_ref[:, 0:w2], h0_ref[:, w2:w]))
        hfin_ref[:, 0:w2] = hr
        hfin_ref[:, w2:w] = hi

    for g in range(gb):
        hp = jnp.concatenate([s_ref[:, :, g * LANE:(g + 1) * LANE].reshape(rows, LANE),
                              s_ref[:, :, w2 + g * LANE:w2 + (g + 1) * LANE].reshape(rows, LANE)], axis=-1)
        y = _dot(u_ref[:, g * gw:(g + 1) * gw], kfull_ref[g]) + _dot(hp.astype(BF16), kout_ref[g])
        y_ref[:, g * gw:(g + 1) * gw] = y.astype(y_ref.dtype)


def _s5(u_t, kst, kfull, kout, ar, ai, h0, *, nseq, gb):
    rows, gcols = u_t.shape
    ngrp = kst.shape[0]
    nchunk = rows // nseq
    gw = 2 * LANE
    kern = functools.partial(_s5_kernel, gb=gb, nseq=nseq, nchunk=nchunk)
    kspec = pl.BlockSpec((gb, gw, gw), lambda i: (i, 0, 0))
    return pl.pallas_call(
        kern,
        grid=(ngrp // gb,),
        in_specs=[
            pl.BlockSpec((rows, gb * gw), lambda i: (0, i)),
            kspec, kspec, kspec,
            pl.BlockSpec((1, gb * LANE), lambda i: (0, i)),
            pl.BlockSpec((1, gb * LANE), lambda i: (0, i)),
            pl.BlockSpec((nseq, gb * gw), lambda i: (0, i)),
        ],
        out_specs=[pl.BlockSpec((rows, gb * gw), lambda i: (0, i)),
                   pl.BlockSpec((nseq, gb * gw), lambda i: (0, i))],
        out_shape=[jax.ShapeDtypeStruct((rows, gcols), F32),
                   jax.ShapeDtypeStruct((nseq, gcols), F32)],
        scratch_shapes=[pltpu.VMEM((rows // SUBLANE, SUBLANE, gb * gw), F32)],
        name="s5_seq%d" % nseq,
        compiler_params=_cparams(("arbitrary",)),
    )(u_t, kst, kfull, kout, ar, ai, h0)


def _s5_weights(lam_re, lam_im, log_dt, b_re, b_im, c_re, c_im, d_skip, gb):
    t_len = S5_CHUNK
    _, ngrp, p_len = lam_re.shape
    h_len = b_re.shape[-1]
    dt = jnp.exp(log_dt)[..., None]
    e = jnp.exp(lam_re * dt)
    lbr = e * jnp.cos(lam_im * dt)
    lbi = e * jnp.sin(lam_im * dt)
    den = lam_re * lam_re + lam_im * lam_im
    fr = ((lbr - 1.0) * lam_re + lbi * lam_im) / den
    fi = (lbi * lam_re - (lbr - 1.0) * lam_im) / den
    br = fr[..., None] * b_re - fi[..., None] * b_im
    bi = fr[..., None] * b_im + fi[..., None] * b_re
    pr = [jnp.ones_like(lbr)]
    pi = [jnp.zeros_like(lbr)]
    for _ in range(t_len):
        pr.append(pr[-1] * lbr - pi[-1] * lbi)
        pi.append(pr[-2] * lbi + pi[-1] * lbr)
    pwr = jnp.stack(pr)
    pwi = jnp.stack(pi)

    tt = np.arange(t_len)
    er = jnp.stack([pwr[t_len - 1 - tt, 0], pwr[tt, 1]], axis=1)
    ei = jnp.stack([pwi[t_len - 1 - tt, 0], pwi[tt, 1]], axis=1)
    ks_re = er[..., None] * br[None] - ei[..., None] * bi[None]
    ks_im = er[..., None] * bi[None] + ei[..., None] * br[None]

    def st_cols(x):
        return jnp.transpose(x, (2, 0, 4, 1, 3)).reshape(ngrp, t_len * h_len, 2 * p_len)

    kst = jnp.concatenate([st_cols(ks_re), st_cols(ks_im)], axis=-1)

    qr = jnp.stack([pwr[tt + 1, 0], pwr[t_len - tt, 1]], axis=1)
    qi = jnp.stack([pwi[tt + 1, 0], pwi[t_len - tt, 1]], axis=1)
    cqr = c_re[None] * qr[:, :, :, None, :] - c_im[None] * qi[:, :, :, None, :]
    cqi = c_re[None] * qi[:, :, :, None, :] + c_im[None] * qr[:, :, :, None, :]

    def out_rows(x):
        return jnp.transpose(x, (2, 1, 4, 0, 3)).reshape(ngrp, 2 * p_len, t_len * h_len)

    kout = jnp.concatenate([out_rows(cqr), out_rows(-cqi)], axis=1)

    cpr = c_re[None] * pwr[:t_len, :, :, None, :] - c_im[None] * pwi[:t_len, :, :, None, :]
    cpi = c_re[None] * pwi[:t_len, :, :, None, :] + c_im[None] * pwr[:t_len, :, :, None, :]
    m = (jnp.einsum('kdghp,dgpj->kdghj', cpr, br, precision=HI)
         - jnp.einsum('kdghp,dgpj->kdghj', cpi, bi, precision=HI))
    mz = jnp.concatenate([m, jnp.zeros_like(m[:1])], axis=0)
    lag = tt[None, :] - tt[:, None]
    f_idx = np.where(lag >= 0, lag, t_len)
    b_idx = np.where(lag <= 0, -lag, t_len)
    full = mz[f_idx, 0] + mz[b_idx, 1]
    eye_t = jnp.asarray(np.eye(t_len, dtype=np.float32))
    eye_h = jnp.asarray(np.eye(h_len, dtype=np.float32))
    full = full + (eye_t[:, :, None, None, None] * d_skip[None, None, :, :, None] * eye_h[None, None, None])
    kfull = jnp.transpose(full, (2, 0, 4, 1, 3)).reshape(ngrp, t_len * h_len, t_len * h_len)

    ar = jnp.transpose(pwr[t_len], (1, 0, 2)).reshape(1, ngrp * 2 * p_len)
    ai = jnp.transpose(pwi[t_len], (1, 0, 2)).reshape(1, ngrp * 2 * p_len)
    return kst.astype(BF16), kfull.astype(BF16), kout.astype(BF16), ar, ai


def _state_to_cols(s_re, s_im, gb):
    b, _, ngrp, p_len = s_re.shape
    def blk(x):
        return jnp.transpose(x, (0, 2, 1, 3)).reshape(b, ngrp // gb, 1, gb * 2 * p_len)
    return jnp.concatenate([blk(s_re), blk(s_im)], axis=2).reshape(b, ngrp * 4 * p_len)


def _cols_to_state(cols, ngrp, p_len, gb):
    b = cols.shape[0]
    x = cols.reshape(b, ngrp // gb, 2, gb, 2, p_len)
    x = jnp.transpose(x, (2, 0, 4, 1, 3, 5)).reshape(2, b, 2, ngrp, p_len)
    return x[0], x[1]


def _to_chunks(u, nseq, seq_len, ngrp, h_len):
    nchunk = seq_len // S5_CHUNK
    x = u.reshape(nseq, nchunk, S5_CHUNK, ngrp, h_len)
    return jnp.transpose(x, (1, 0, 3, 2, 4)).reshape(nchunk * nseq, ngrp * S5_CHUNK * h_len)


def _from_chunks(y, nseq, seq_len, ngrp, h_len):
    nchunk = seq_len // S5_CHUNK
    x = y.reshape(nchunk, nseq, ngrp, S5_CHUNK, h_len)
    return jnp.transpose(x, (1, 0, 3, 2, 4)).reshape(nseq * seq_len, ngrp * h_len)


def _ctx_attn_kernel(q_ref, k_ref, v_ref, o_ref, *, nh, dh):
    scale = dh ** -0.5
    for h in range(nh):
        sl = slice(h * dh, (h + 1) * dh)
        s = _dot_nt(q_ref[:, sl], k_ref[:, sl].astype(BF16)) * scale
        e = jnp.exp(s - jnp.max(s, axis=-1, keepdims=True))
        o = _dot(e.astype(BF16), v_ref[:, sl].astype(BF16)) / jnp.sum(e, axis=-1, keepdims=True)
        o_ref[:, sl] = o.astype(o_ref.dtype)


def _ctx_attn(q, k, v, *, nbatch, seq, nh):
    dcols = q.shape[1]
    kern = functools.partial(_ctx_attn_kernel, nh=nh, dh=dcols // nh)
    blk = pl.BlockSpec((seq, dcols), lambda b: (b, 0))
    return pl.pallas_call(
        kern,
        grid=(nbatch,),
        in_specs=[blk, blk, blk],
        out_specs=blk,
        out_shape=jax.ShapeDtypeStruct((nbatch * seq, dcols), BF16),
        name="ctx_attn",
        compiler_params=_cparams(("arbitrary",)),
    )(q, k, v)


def _na_window_start(i, rows, win_r):
    nkr = win_r + NA_QROWS
    first = jnp.clip(NA_QROWS * i - win_r // 2, 0, rows - win_r)
    return jnp.minimum(first, rows - nkr)


def _na_kernel(q_ref, k_ref, v_ref, kc_ref, vc_ref, bias_ref, o_ref, *, rows, win_r, dh):
    i = pl.program_id(2)
    nk = (win_r + NA_QROWS) * GRID_W
    start = pl.multiple_of(_na_window_start(i, rows, win_r) * GRID_W, GRID_W)
    scale = dh ** -0.5
    q = q_ref[...]
    kw = k_ref[pl.ds(start, nk), :].astype(BF16)
    vw = v_ref[pl.ds(start, nk), :].astype(BF16)
    s_loc = _dot_nt(q, kw) * scale + bias_ref[0, 0]
    s_ctx = _dot_nt(q, kc_ref[...].astype(BF16)) * scale
    m = jnp.maximum(jnp.max(s_loc, axis=-1, keepdims=True), jnp.max(s_ctx, axis=-1, keepdims=True))
    e_loc = jnp.exp(s_loc - m)
    e_ctx = jnp.exp(s_ctx - m)
    den = jnp.sum(e_loc, axis=-1, keepdims=True) + jnp.sum(e_ctx, axis=-1, keepdims=True)
    o = (_dot(e_loc.astype(BF16), vw) + _dot(e_ctx.astype(BF16), vc_ref[...].astype(BF16))) / den
    o_ref[...] = o.astype(o_ref.dtype)


def _na_bias(rpb, rows):
    nh, nir, nic = rpb.shape
    win_r = (nir + 1) // 2
    win_c = (nic + 1) // 2
    wr = min(win_r, rows)
    nkr = wr + NA_QROWS
    nblk = rows // NA_QROWS
    col = np.arange(GRID_W)
    c0 = np.clip(col - win_c // 2, 0, GRID_W - win_c)
    col_ok = (col[None, :] >= c0[:, None]) & (col[None, :] < c0[:, None] + win_c)
    ic = np.clip(col[None, :] - col[:, None] + (win_c - 1), 0, 2 * win_c - 2)

    def row_pattern(i):
        start = min(int(np.clip(NA_QROWS * i - wr // 2, 0, rows - wr)), rows - nkr)
        r = NA_QROWS * i + np.arange(NA_QROWS)
        kr = start + np.arange(nkr)
        kstart = np.clip(r - wr // 2, 0, rows - wr)
        row_ok = (kr[None, :] >= kstart[:, None]) & (kr[None, :] < kstart[:, None] + wr)
        ir = np.clip(kr[None, :] - r[:, None] + (win_r - 1), 0, nir - 1)
        assert row_ok.sum(axis=1).min() == wr
        return row_ok, np.where(row_ok, ir, 0)

    for i in range(2, nblk - 1):
        assert all(np.array_equal(a, b) for a, b in zip(row_pattern(i), row_pattern(1)))

    tables = []
    for i in (0, 1, nblk - 1):
        row_ok, ir = row_pattern(i)
        ok = row_ok[:, None, :, None] & col_ok[None, :, None, :]
        ir_b = np.broadcast_to(ir[:, None, :, None], ok.shape)
        ic_b = np.broadcast_to(ic[None, :, None, :], ok.shape)
        vals = rpb.astype(F32)[:, ir_b, ic_b]
        vals = jnp.where(jnp.asarray(ok)[None], vals, NEG_INF)
        tables.append(vals.reshape(nh, NA_QROWS * GRID_W, nkr * GRID_W))
    return jnp.stack(tables), wr


def _na_attn(q, k, v, kc, vc, bias, *, nbatch, seq, past, nh, row0, win_r):
    dcols = q.shape[1]
    dh = dcols // nh
    rows = seq // GRID_W
    tq = NA_QROWS * GRID_W
    nblk = rows // NA_QROWS
    assert row0 % seq == 0 and rows % NA_QROWS == 0 and rows >= win_r + NA_QROWS
    kern = functools.partial(_na_kernel, rows=rows, win_r=win_r, dh=dh)
    qoff = row0 // tq
    boff = row0 // seq

    def cls(i):
        return jnp.where(i == 0, 0, jnp.where(i == nblk - 1, 2, 1))

    kv_spec = pl.BlockSpec((seq, dh), lambda b, h, i: (boff + b, h))
    c_spec = pl.BlockSpec((past, dh), lambda b, h, i: (b, h))
    return pl.pallas_call(
        kern,
        grid=(nbatch, nh, nblk),
        in_specs=[
            pl.BlockSpec((tq, dh), lambda b, h, i: (qoff + b * nblk + i, h)),
            kv_spec, kv_spec, c_spec, c_spec,
            pl.BlockSpec((1, 1) + bias.shape[2:], lambda b, h, i: (cls(i), h, 0, 0)),
        ],
        out_specs=pl.BlockSpec((tq, dh), lambda b, h, i: (b * nblk + i, h)),
        out_shape=jax.ShapeDtypeStruct((nbatch * seq, dcols), BF16),
        name="na_attn",
        compiler_params=_cparams(("arbitrary", "arbitrary", "arbitrary")),
    )(q, k, v, kc, vc, bias)


def _even_out_kernel(ys_ref, yb_ref, x_ref, m_ref, g_ref, wglu_ref, woa_ref, wob_ref, o_ref):
    y = jax.nn.gelu(ys_ref[...])
    gate = jax.nn.sigmoid(_dot(y.astype(BF16), wglu_ref[...]))
    ya = (y * gate).astype(BF16)
    o = _dot(ya, woa_ref[...]) + _dot(yb_ref[...], wob_ref[...])
    r = _rms(o) * g_ref[3:4, :]
    o_ref[...] = x_ref[...] + m_ref[0, 5:6, :] * r


def _even_out(ys, yb, x, mod_l, g_l, w_glu, w_out, *, gid, tm):
    n, d = x.shape
    da = ys.shape[1]
    db = yb.shape[1]
    assert da == db
    return pl.pallas_call(
        _even_out_kernel,
        grid=(n // tm,),
        in_specs=[
            pl.BlockSpec((tm, da), lambda i: (i, 0)),
            pl.BlockSpec((tm, db), lambda i: (i, 0)),
            pl.BlockSpec((tm, d), lambda i: (i, 0)),
            pl.BlockSpec((1, N_MOD, d), lambda i: (gid(i), 0, 0)),
            pl.BlockSpec(g_l.shape, lambda i: (0, 0)),
            _resident((da, da), lambda i: (0, 0)),
            _resident((da, d), lambda i: (0, 0)),
            _resident((db, d), lambda i: (1, 0)),
        ],
        out_specs=pl.BlockSpec((tm, d), lambda i: (i, 0)),
        out_shape=jax.ShapeDtypeStruct((n, d), F32),
        name="even_out",
        compiler_params=_cparams(("arbitrary",)),
    )(ys, yb, x, mod_l, g_l, w_glu, w_out, w_out)


def _odd_in_kernel(x_ref, m_ref, g_ref, wb_ref, wc_ref, wx_ref, b_ref, p_ref, h_ref):
    j = pl.program_id(1)

    @pl.when(j == 0)
    def _():
        h_ref[...] = _modnorm(x_ref[...], g_ref[2:3, :], m_ref[0, 3:4, :], m_ref[0, 4:5, :]).astype(BF16)

    h = h_ref[...]
    b_ref[...] = _dot(h, wb_ref[...])
    p_ref[...] = _dot(h, wc_ref[...]) * _dot(h, wx_ref[...])


def _odd_in(x, mod_l, g_l, w, *, gid, tm):
    n, d = x.shape
    dc = w.shape[1] // 3
    tn = _tile(dc, 512)
    nj = dc // tn
    blk = pl.BlockSpec((tm, tn), lambda i, j: (i, j))
    return pl.pallas_call(
        _odd_in_kernel,
        grid=(n // tm, nj),
        in_specs=[
            pl.BlockSpec((tm, d), lambda i, j: (i, 0)),
            pl.BlockSpec((1, N_MOD, d), lambda i, j: (gid(i), 0, 0)),
            pl.BlockSpec(g_l.shape, lambda i, j: (0, 0)),
            pl.BlockSpec((d, tn), lambda i, j: (0, j)),
            pl.BlockSpec((d, tn), lambda i, j: (0, j + nj)),
            pl.BlockSpec((d, tn), lambda i, j: (0, j + 2 * nj)),
        ],
        out_specs=[blk, blk],
        out_shape=[jax.ShapeDtypeStruct((n, dc), F32), jax.ShapeDtypeStruct((n, dc), F32)],
        scratch_shapes=[pltpu.VMEM((tm, d), BF16)],
        name="odd_in",
        compiler_params=_cparams(("arbitrary", "arbitrary")),
    )(x, mod_l, g_l, w, w, w)


def _odd_out_kernel(b_ref, p_ref, pp_ref, pn_ref, cw_ref, x_ref, m_ref, g_ref, w_ref, o_ref,
                    *, tm, ctx_rows, ctx_seq, lat_seq):
    i = pl.program_id(0)
    p = p_ref[...]
    row = lax.broadcasted_iota(jnp.int32, (tm, 1), 0)
    grow = i * tm + row
    is_ctx = grow < ctx_rows
    pos = jnp.where(is_ctx, grow % ctx_seq, (grow - ctx_rows) % lat_seq)
    last = jnp.where(is_ctx, ctx_seq - 1, lat_seq - 1)
    prev = pltpu.roll(p, 1, axis=0)
    prev = jnp.where(row == 0, pp_ref[SUBLANE - 1:SUBLANE, :], prev)
    prev = jnp.where(pos == 0, 0.0, prev)
    nxt = pltpu.roll(p, tm - 1, axis=0)
    nxt = jnp.where(row == tm - 1, pn_ref[0:1, :], nxt)
    nxt = jnp.where(pos == last, 0.0, nxt)
    y = cw_ref[0:1, :] * prev + cw_ref[1:2, :] * p + cw_ref[2:3, :] * nxt
    v = (b_ref[...] * y).astype(BF16)
    o = _dot(v, w_ref[...])
    r = _rms(o) * g_ref[3:4, :]
    o_ref[...] = x_ref[...] + m_ref[0, 5:6, :] * r


def _odd_out(b, p, conv_w, x, mod_l, g_l, w, *, gid, tm, ctx_rows, ctx_seq, lat_seq):
    n, d = x.shape
    dc = b.shape[1]
    assert conv_w.shape[0] == 3
    kern = functools.partial(_odd_out_kernel, tm=tm, ctx_rows=ctx_rows, ctx_seq=ctx_seq, lat_seq=lat_seq)
    tpb = tm // SUBLANE
    nb8 = n // SUBLANE
    return pl.pallas_call(
        kern,
        grid=(n // tm,),
        in_specs=[
            pl.BlockSpec((tm, dc), lambda i: (i, 0)),
            pl.BlockSpec((tm, dc), lambda i: (i, 0)),
            pl.BlockSpec((SUBLANE, dc), lambda i: (jnp.maximum(i * tpb - 1, 0), 0)),
            pl.BlockSpec((SUBLANE, dc), lambda i: (jnp.minimum((i + 1) * tpb, nb8 - 1), 0)),
            pl.BlockSpec(conv_w.shape, lambda i: (0, 0)),
            pl.BlockSpec((tm, d), lambda i: (i, 0)),
            pl.BlockSpec((1, N_MOD, d), lambda i: (gid(i), 0, 0)),
            pl.BlockSpec(g_l.shape, lambda i: (0, 0)),
            _resident((dc, d), lambda i: (0, 0)),
        ],
        out_specs=pl.BlockSpec((tm, d), lambda i: (i, 0)),
        out_shape=jax.ShapeDtypeStruct((n, d), F32),
        name="odd_out",
        compiler_params=_cparams(("arbitrary",)),
    )(b, p, p, p, conv_w, x, mod_l, g_l, w)


def kernel(x_prompt, x_sample, cache_k, cache_v, state_s5_re, state_s5_im, c, c_ctx, norm_g, w_mod, b_mod,
           w_ffn_in, w_ffn_out, w_in_even, w_out_even, s5_lam_re, s5_lam_im, s5_log_dt, s5_b_re, s5_b_im,
           s5_c_re, s5_c_im, s5_d, w_glu, na_rpb, w_in_conv, conv_w, w_out_conv):
    nb_c, seq_c, d = x_prompt.shape
    nb_l, seq_l, _ = x_sample.shape
    depth = norm_g.shape[0]
    past, nh = cache_k.shape[2], cache_k.shape[3]
    ngrp, p_len = state_s5_re.shape[3], state_s5_re.shape[4]
    h_len = s5_b_re.shape[-1]
    d_a = ngrp * h_len
    ctx_rows = nb_c * seq_c
    n = ctx_rows + nb_l * seq_l
    tm = _tile(math.gcd(ctx_rows, seq_l), 512)
    gb = _tile(ngrp, 8)
    assert S5_CHUNK * h_len == 2 * LANE and 2 * p_len == LANE

    tm_c = _tile(tm, 256)

    def group_of_tile(rows_per_tile):
        def gid(i):
            r0 = i * rows_per_tile
            return jnp.where(r0 < ctx_rows, 0, 1 + jnp.maximum(r0 - ctx_rows, 0) // seq_l)
        return gid

    gid = group_of_tile(tm)
    gid_c = group_of_tile(tm_c)

    ng = 1 + nb_l
    ng8 = -(-ng // SUBLANE) * SUBLANE
    cvec = jnp.concatenate([c_ctx[None], c, jnp.zeros((ng8 - ng, d), F32)], axis=0)
    mod = _modulation(cvec, w_mod, b_mod).reshape(depth, ng8, N_MOD, d)

    x = jnp.concatenate([x_prompt.reshape(ctx_rows, d), x_sample.reshape(nb_l * seq_l, d)], axis=0)

    new_k, new_v, new_sre, new_sim = [], [], [], []
    for l in range(depth):
        g_l = norm_g[l]
        mod_l = mod[l]
        x = _ffn(x, mod_l, g_l, w_ffn_in[l, 0].astype(BF16), w_ffn_out[l, 0].astype(BF16),
                 mi=0, gi=0, gid=gid, tm=tm)
        if l % 2 == 0:
            e = l // 2
            u, q, k, v = _even_in(x, mod_l, g_l, w_in_even[e].astype(BF16), gid=gid, tm=tm)
            new_k.append(k[:ctx_rows].reshape(nb_c, seq_c, nh, -1))
            new_v.append(v[:ctx_rows].reshape(nb_c, seq_c, nh, -1))
            kst, kfull, kout, ar, ai = _s5_weights(s5_lam_re[e], s5_lam_im[e], s5_log_dt[e], s5_b_re[e],
                                                   s5_b_im[e], s5_c_re[e], s5_c_im[e], s5_d[e], gb)
            h0_c = jnp.zeros((nb_c, ngrp * 4 * p_len), F32)
            ys_c, hfin = _s5(_to_chunks(u[:ctx_rows], nb_c, seq_c, ngrp, h_len), kst, kfull, kout, ar, ai,
                             h0_c, nseq=nb_c, gb=gb)
            fin_re, fin_im = _cols_to_state(hfin, ngrp, p_len, gb)
            new_sre.append(fin_re)
            new_sim.append(fin_im)
            h0_l = _state_to_cols(state_s5_re[:, e], state_s5_im[:, e], gb)
            ys_l, _ = _s5(_to_chunks(u[ctx_rows:], nb_l, seq_l, ngrp, h_len), kst, kfull, kout, ar, ai,
                          h0_l, nseq=nb_l, gb=gb)
            ys = jnp.concatenate([_from_chunks(ys_c, nb_c, seq_c, ngrp, h_len),
                                  _from_chunks(ys_l, nb_l, seq_l, ngrp, h_len)], axis=0)
            yb_c = _ctx_attn(q, k, v, nbatch=nb_c, seq=seq_c, nh=nh)
            bias, win_r = _na_bias(na_rpb[e], seq_l // GRID_W)
            yb_l = _na_attn(q, k, v, cache_k[:, e].reshape(nb_l * past, -1),
                            cache_v[:, e].reshape(nb_l * past, -1), bias,
                            nbatch=nb_l, seq=seq_l, past=past, nh=nh, row0=ctx_rows, win_r=win_r)
            yb = jnp.concatenate([yb_c, yb_l], axis=0)
            x = _even_out(ys, yb, x, mod_l, g_l, w_glu[e].astype(BF16), w_out_even[e].astype(BF16),
                          gid=gid, tm=tm)
        else:
            o = l // 2
            b, p = _odd_in(x, mod_l, g_l, w_in_conv[o].astype(BF16), gid=gid, tm=tm)
            x = _odd_out(b, p, conv_w[o], x, mod_l, g_l, w_out_conv[o].astype(BF16), gid=gid_c, tm=tm_c,
                         ctx_rows=ctx_rows, ctx_seq=seq_c, lat_seq=seq_l)
        x = _ffn(x, mod_l, g_l, w_ffn_in[l, 1].astype(BF16), w_ffn_out[l, 1].astype(BF16),
                 mi=6, gi=4, gid=gid, tm=tm)

    y_prompt = x[:ctx_rows].reshape(nb_c, seq_c, d)
    y_sample = x[ctx_rows:].reshape(nb_l, seq_l, d)
    return (y_prompt, y_sample, jnp.stack(new_k, axis=1), jnp.stack(new_v, axis=1),
            jnp.stack(new_sre, axis=1), jnp.stack(new_sim, axis=1))
```

```python
import functools
import math

import numpy as np
import jax
import jax.numpy as jnp
from jax import lax
from jax.experimental import pallas as pl
from jax.experimental.pallas import tpu as pltpu

F32 = jnp.float32
BF16 = jnp.bfloat16

NORM_EPS = 1e-6
N_MOD = 9
GRID_W = 64
NEG_INF = -1e30
S5_CHUNK = 8
NA_QROWS = 4
LANE = 128
SUBLANE = 8
VMEM_LIMIT_BYTES = 56 * 1024 * 1024
HI = lax.Precision.HIGHEST


def _cparams(sem):
    return pltpu.CompilerParams(dimension_semantics=sem, vmem_limit_bytes=VMEM_LIMIT_BYTES)


def _tile(n, pref):
    t = min(n, pref)
    while n % t:
        t //= 2
    return t


def _resident(block_shape, index_map):
    return pl.BlockSpec(block_shape, index_map, pipeline_mode=pl.Buffered(1))


def _rms(x):
    return x * lax.rsqrt(jnp.mean(x * x, axis=-1, keepdims=True) + NORM_EPS)


def _modnorm(x, g, shift, scale):
    return (_rms(x) * g) * (1.0 + scale) + shift


def _dot(a, b):
    return jnp.dot(a, b, preferred_element_type=F32)


def _dot_nt(a, b):
    return lax.dot_general(a, b, (((1,), (1,)), ((), ())), preferred_element_type=F32)


def _mod_kernel(c_ref, w_ref, b_ref, o_ref):
    c = c_ref[...]
    s = (c * jax.nn.sigmoid(c)).astype(BF16)
    o_ref[0] = _dot(s, w_ref[0].astype(BF16)) + b_ref[0]


def _modulation(cvec, w_mod, b_mod):
    depth, d, nd = w_mod.shape
    ng = cvec.shape[0]
    tn = _tile(nd, 1024)
    return pl.pallas_call(
        _mod_kernel,
        grid=(depth, nd // tn),
        in_specs=[
            pl.BlockSpec((ng, d), lambda l, j: (0, 0)),
            pl.BlockSpec((1, d, tn), lambda l, j: (l, 0, j)),
            pl.BlockSpec((1, 1, tn), lambda l, j: (l, 0, j)),
        ],
        out_specs=pl.BlockSpec((1, ng, tn), lambda l, j: (l, 0, j)),
        out_shape=jax.ShapeDtypeStruct((depth, ng, nd), F32),
        name="modulation",
        compiler_params=_cparams(("arbitrary", "arbitrary")),
    )(cvec, w_mod, b_mod.reshape(depth, 1, nd))


def _ffn_kernel(x_ref, m_ref, g_ref, wg_ref, wu_ref, wo_ref, o_ref, h_ref, *, mi, gi, nj):
    j = pl.program_id(1)

    @pl.when(j == 0)
    def _():
        h_ref[...] = _modnorm(x_ref[...], g_ref[gi:gi + 1, :], m_ref[0, mi:mi + 1, :],
                              m_ref[0, mi + 1:mi + 2, :]).astype(BF16)
        o_ref[...] = jnp.zeros_like(o_ref)

    h = h_ref[...]
    a = _dot(h, wg_ref[...])
    u = _dot(h, wu_ref[...])
    hid = (a * jax.nn.sigmoid(a) * u).astype(BF16)
    o_ref[...] += _dot(hid, wo_ref[...])

    @pl.when(j == nj - 1)
    def _():
        r = _rms(o_ref[...]) * g_ref[gi + 1:gi + 2, :]
        o_ref[...] = x_ref[...] + 0.5 * m_ref[0, mi + 2:mi + 3, :] * r


def _ffn(x, mod_l, g_l, w_in, w_out, *, mi, gi, gid, tm):
    n, d = x.shape
    dff = w_out.shape[0]
    tf = _tile(dff, 512)
    nj = dff // tf
    kern = functools.partial(_ffn_kernel, mi=mi, gi=gi, nj=nj)
    return pl.pallas_call(
        kern,
        grid=(n // tm, nj),
        in_specs=[
            pl.BlockSpec((tm, d), lambda i, j: (i, 0)),
            pl.BlockSpec((1, N_MOD, d), lambda i, j: (gid(i), 0, 0)),
            pl.BlockSpec(g_l.shape, lambda i, j: (0, 0)),
            pl.BlockSpec((d, tf), lambda i, j: (0, j)),
            pl.BlockSpec((d, tf), lambda i, j: (0, j + nj)),
            pl.BlockSpec((tf, d), lambda i, j: (j, 0)),
        ],
        out_specs=pl.BlockSpec((tm, d), lambda i, j: (i, 0)),
        out_shape=jax.ShapeDtypeStruct((n, d), F32),
        scratch_shapes=[pltpu.VMEM((tm, d), BF16)],
        name="ffn",
        compiler_params=_cparams(("arbitrary", "arbitrary")),
    )(x, mod_l, g_l, w_in, w_in, w_out)


def _even_in_kernel(x_ref, m_ref, g_ref, w_ref, z_ref, h_ref):
    @pl.when(pl.program_id(1) == 0)
    def _():
        h_ref[...] = _modnorm(x_ref[...], g_ref[2:3, :], m_ref[0, 3:4, :], m_ref[0, 4:5, :]).astype(BF16)

    z_ref[...] = _dot(h_ref[...], w_ref[...])


def _even_in(x, mod_l, g_l, w, *, gid, tm):
    n, d = x.shape
    nz = w.shape[1]
    tn = _tile(nz, 1024)
    return pl.pallas_call(
        _even_in_kernel,
        grid=(n // tm, nz // tn),
        in_specs=[
            pl.BlockSpec((tm, d), lambda i, j: (i, 0)),
            pl.BlockSpec((1, N_MOD, d), lambda i, j: (gid(i), 0, 0)),
            pl.BlockSpec(g_l.shape, lambda i, j: (0, 0)),
            pl.BlockSpec((d, tn), lambda i, j: (0, j)),
        ],
        out_specs=pl.BlockSpec((tm, tn), lambda i, j: (i, j)),
        out_shape=jax.ShapeDtypeStruct((n, nz), F32),
        scratch_shapes=[pltpu.VMEM((tm, d), BF16)],
        name="even_in",
        compiler_params=_cparams(("arbitrary", "arbitrary")),
    )(x, mod_l, g_l, w)


def _s5_kernel(u_ref, kst_ref, kfull_ref, kout_ref, ar_ref, ai_ref, h0_ref, y_ref, hfin_ref, s_ref,
               *, nseq, nchunk, rt):
    t_len, rows, _ = u_ref.shape
    w = s_ref.shape[2]
    w2 = w // 2

    def chunk_lhs(r0):
        return jnp.concatenate([u_ref[t, r0:r0 + rt, :] for t in range(t_len)], axis=-1)

    for r0 in range(0, rows, rt):
        res = _dot(chunk_lhs(r0), kst_ref[0])
        s_ref[r0 // SUBLANE:(r0 + rt) // SUBLANE] = res.reshape(rt // SUBLANE, SUBLANE, w)

    ar = ar_ref[...]
    ai = ai_ref[...]

    def step(hr, hi, sr, si):
        return ar * hr - ai * hi + sr, ar * hi + ai * hr + si

    if nseq % SUBLANE == 0:
        nb = nseq // SUBLANE
        d0 = (lax.broadcasted_iota(jnp.int32, (nb, SUBLANE, w2), 2) & (LANE // 2)) == 0
        hr0 = h0_ref[:, 0:w2].reshape(nb, SUBLANE, w2)
        hi0 = h0_ref[:, w2:w].reshape(nb, SUBLANE, w2)

        def body(k, carry):
            hr, hi = carry
            k1 = nchunk - 1 - k
            b0r = s_ref[pl.ds(k * nb, nb), :, 0:w2]
            b0i = s_ref[pl.ds(k * nb, nb), :, w2:w]
            b1r = s_ref[pl.ds(k1 * nb, nb), :, 0:w2]
            b1i = s_ref[pl.ds(k1 * nb, nb), :, w2:w]
            s_ref[pl.ds(k * nb, nb), :, 0:w2] = jnp.where(d0, hr, b0r)
            s_ref[pl.ds(k * nb, nb), :, w2:w] = jnp.where(d0, hi, b0i)
            s_ref[pl.ds(k1 * nb, nb), :, 0:w2] = jnp.where(d0, b1r, hr)
            s_ref[pl.ds(k1 * nb, nb), :, w2:w] = jnp.where(d0, b1i, hi)
            return step(hr, hi, jnp.where(d0, b0r, b1r), jnp.where(d0, b0i, b1i))

        assert nchunk % 2 == 0
        hr, hi = lax.fori_loop(0, nchunk, body, (hr0, hi0))
        hfin_ref[:, 0:w2] = hr.reshape(nseq, w2)
        hfin_ref[:, w2:w] = hi.reshape(nseq, w2)
    else:
        half = SUBLANE // 2
        assert nseq == half and nchunk % 4 == 0
        nblk = nchunk // 2
        d0 = (lax.broadcasted_iota(jnp.int32, (half, w2), 1) & (LANE // 2)) == 0
        lo = slice(0, half)
        up = slice(half, SUBLANE)

        def body(k, carry):
            hr, hi = carry
            k1 = nblk - 1 - k
            b0r = s_ref[k, :, 0:w2]
            b0i = s_ref[k, :, w2:w]
            b1r = s_ref[k1, :, 0:w2]
            b1i = s_ref[k1, :, w2:w]
            for fa, ba in ((lo, up), (up, lo)):
                s_ref[k, fa, 0:w2] = jnp.where(d0, hr, b0r[fa])
                s_ref[k, fa, w2:w] = jnp.where(d0, hi, b0i[fa])
                s_ref[k1, ba, 0:w2] = jnp.where(d0, b1r[ba], hr)
                s_ref[k1, ba, w2:w] = jnp.where(d0, b1i[ba], hi)
                hr, hi = step(hr, hi, jnp.where(d0, b0r[fa], b1r[ba]), jnp.where(d0, b0i[fa], b1i[ba]))
            return hr, hi

        hr, hi = lax.fori_loop(0, nblk, body, (h0_ref[:, 0:w2], h0_ref[:, w2:w]))
        hfin_ref[:, 0:w2] = hr
        hfin_ref[:, w2:w] = hi

    for r0 in range(0, rows, rt):
        hp = s_ref[r0 // SUBLANE:(r0 + rt) // SUBLANE].reshape(rt, w).astype(BF16)
        y = _dot(chunk_lhs(r0), kfull_ref[0]) + _dot(hp, kout_ref[0])
        for t in range(t_len):
            y_ref[t, r0:r0 + rt, :] = y[:, t * LANE:(t + 1) * LANE].astype(y_ref.dtype)


def _s5(u_t, kst, kfull, kout, ar, ai, h0, *, nseq):
    t_len, rows, _ = u_t.shape
    nblk, kdim, sdim = kst.shape
    nchunk = rows // nseq
    kern = functools.partial(_s5_kernel, nseq=nseq, nchunk=nchunk, rt=_tile(rows, 512))
    return pl.pallas_call(
        kern,
        grid=(nblk,),
        in_specs=[
            pl.BlockSpec((t_len, rows, LANE), lambda i: (0, 0, i)),
            _resident((1, kdim, sdim), lambda i: (i, 0, 0)),
            _resident((1, kdim, kdim), lambda i: (i, 0, 0)),
            _resident((1, sdim, kdim), lambda i: (i, 0, 0)),
            pl.BlockSpec((1, sdim // 2), lambda i: (0, i)),
            pl.BlockSpec((1, sdim // 2), lambda i: (0, i)),
            pl.BlockSpec((nseq, sdim), lambda i: (0, i)),
        ],
        out_specs=[pl.BlockSpec((t_len, rows, LANE), lambda i: (0, 0, i)),
                   pl.BlockSpec((nseq, sdim), lambda i: (0, i))],
        out_shape=[jax.ShapeDtypeStruct(u_t.shape, BF16),
                   jax.ShapeDtypeStruct((nseq, nblk * sdim), F32)],
        scratch_shapes=[pltpu.VMEM((rows // SUBLANE, SUBLANE, sdim), F32)],
        name="s5_seq%d" % nseq,
        compiler_params=_cparams(("arbitrary",)),
    )(u_t, kst, kfull, kout, ar, ai, h0)


def _s5_weights(lam_re, lam_im, log_dt, b_re, b_im, c_re, c_im, d_skip, gb):
    t_len = S5_CHUNK
    _, ngrp, p_len = lam_re.shape
    h_len = b_re.shape[-1]
    dt = jnp.exp(log_dt)[..., None]
    e = jnp.exp(lam_re * dt)
    lbr = e * jnp.cos(lam_im * dt)
    lbi = e * jnp.sin(lam_im * dt)
    den = lam_re * lam_re + lam_im * lam_im
    fr = ((lbr - 1.0) * lam_re + lbi * lam_im) / den
    fi = (lbi * lam_re - (lbr - 1.0) * lam_im) / den
    br = fr[..., None] * b_re - fi[..., None] * b_im
    bi = fr[..., None] * b_im + fi[..., None] * b_re
    pr = [jnp.ones_like(lbr)]
    pi = [jnp.zeros_like(lbr)]
    for _ in range(t_len):
        pr.append(pr[-1] * lbr - pi[-1] * lbi)
        pi.append(pr[-2] * lbi + pi[-1] * lbr)
    pwr = jnp.stack(pr)
    pwi = jnp.stack(pi)

    tt = np.arange(t_len)
    er = jnp.stack([pwr[t_len - 1 - tt, 0], pwr[tt, 1]], axis=1)
    ei = jnp.stack([pwi[t_len - 1 - tt, 0], pwi[tt, 1]], axis=1)
    ks_re = er[..., None] * br[None] - ei[..., None] * bi[None]
    ks_im = er[..., None] * bi[None] + ei[..., None] * br[None]

    nblk = ngrp // gb
    eye_g = jnp.asarray(np.eye(gb, dtype=np.float32))

    def st_cols(x):
        x = jnp.transpose(x.reshape(t_len, 2, nblk, gb, p_len, h_len), (2, 0, 3, 5, 1, 4))
        x = x[:, :, :, :, None, :, :] * eye_g[None, None, :, None, :, None, None]
        return x.reshape(nblk, t_len * gb * h_len, gb * 2 * p_len)

    kst = jnp.concatenate([st_cols(ks_re), st_cols(ks_im)], axis=-1)

    qr = jnp.stack([pwr[tt + 1, 0], pwr[t_len - tt, 1]], axis=1)
    qi = jnp.stack([pwi[tt + 1, 0], pwi[t_len - tt, 1]], axis=1)
    cqr = c_re[None] * qr[:, :, :, None, :] - c_im[None] * qi[:, :, :, None, :]
    cqi = c_re[None] * qi[:, :, :, None, :] + c_im[None] * qr[:, :, :, None, :]

    def out_rows(x):
        x = jnp.transpose(x.reshape(t_len, 2, nblk, gb, h_len, p_len), (2, 3, 1, 5, 0, 4))
        x = x[:, :, :, :, :, None, :] * eye_g[None, :, None, None, None, :, None]
        return x.reshape(nblk, gb * 2 * p_len, t_len * gb * h_len)

    kout = jnp.concatenate([out_rows(cqr), out_rows(-cqi)], axis=1)

    cpr = c_re[None] * pwr[:t_len, :, :, None, :] - c_im[None] * pwi[:t_len, :, :, None, :]
    cpi = c_re[None] * pwi[:t_len, :, :, None, :] + c_im[None] * pwr[:t_len, :, :, None, :]
    m = (jnp.einsum('kdghp,dgpj->kdghj', cpr, br, precision=HI)
         - jnp.einsum('kdghp,dgpj->kdghj', cpi, bi, precision=HI))
    mz = jnp.concatenate([m, jnp.zeros_like(m[:1])], axis=0)
    lag = tt[None, :] - tt[:, None]
    f_idx = np.where(lag >= 0, lag, t_len)
    b_idx = np.where(lag <= 0, -lag, t_len)
    full = mz[f_idx, 0] + mz[b_idx, 1]
    eye_t = jnp.asarray(np.eye(t_len, dtype=np.float32))
    eye_h = jnp.asarray(np.eye(h_len, dtype=np.float32))
    full = full + (eye_t[:, :, None, None, None] * d_skip[None, None, :, :, None] * eye_h[None, None, None])
    full = jnp.transpose(full.reshape(t_len, t_len, nblk, gb, h_len, h_len), (2, 0, 3, 5, 1, 4))
    full = full[:, :, :, :, :, None, :] * eye_g[None, None, :, None, None, :, None]
    kfull = full.reshape(nblk, t_len * gb * h_len, t_len * gb * h_len)

    ar = jnp.transpose(pwr[t_len], (1, 0, 2)).reshape(1, ngrp * 2 * p_len)
    ai = jnp.transpose(pwi[t_len], (1, 0, 2)).reshape(1, ngrp * 2 * p_len)
    return kst.astype(BF16), kfull.astype(BF16), kout.astype(BF16), ar, ai


def _state_to_cols(s_re, s_im, gb):
    b, _, ngrp, p_len = s_re.shape
    def blk(x):
        return jnp.transpose(x, (0, 2, 1, 3)).reshape(b, ngrp // gb, 1, gb * 2 * p_len)
    return jnp.concatenate([blk(s_re), blk(s_im)], axis=2).reshape(b, ngrp * 4 * p_len)


def _cols_to_state(cols, ngrp, p_len, gb):
    b = cols.shape[0]
    x = cols.reshape(b, ngrp // gb, 2, gb, 2, p_len)
    x = jnp.transpose(x, (2, 0, 4, 1, 3, 5)).reshape(2, b, 2, ngrp, p_len)
    return x[0], x[1]


def _to_chunks(u, nseq, seq_len):
    nchunk = seq_len // S5_CHUNK
    x = u.reshape(nseq, nchunk, S5_CHUNK, u.shape[-1])
    return jnp.transpose(x, (2, 1, 0, 3)).reshape(S5_CHUNK, nchunk * nseq, u.shape[-1]).astype(BF16)


def _from_chunks(y, nseq, seq_len):
    nchunk = seq_len // S5_CHUNK
    x = y.reshape(S5_CHUNK, nchunk, nseq, y.shape[-1])
    return jnp.transpose(x, (2, 1, 0, 3)).reshape(nseq * seq_len, y.shape[-1])


def _ctx_attn_kernel(q_ref, k_ref, v_ref, o_ref, *, nh, dh):
    scale = dh ** -0.5
    for h in range(nh):
        sl = slice(h * dh, (h + 1) * dh)
        s = _dot_nt(q_ref[:, sl].astype(BF16), k_ref[:, sl].astype(BF16)) * scale
        e = jnp.exp(s - jnp.max(s, axis=-1, keepdims=True))
        o = _dot(e.astype(BF16), v_ref[:, sl].astype(BF16)) / jnp.sum(e, axis=-1, keepdims=True)
        o_ref[:, sl] = o.astype(o_ref.dtype)


def _ctx_attn(z, *, nbatch, seq, nh, dcols):
    kern = functools.partial(_ctx_attn_kernel, nh=nh, dh=dcols // nh)
    return pl.pallas_call(
        kern,
        grid=(nbatch,),
        in_specs=[pl.BlockSpec((seq, dcols), lambda b, c=c: (b, c)) for c in (1, 2, 3)],
        out_specs=pl.BlockSpec((seq, dcols), lambda b: (b, 0)),
        out_shape=jax.ShapeDtypeStruct((nbatch * seq, dcols), BF16),
        name="ctx_attn",
        compiler_params=_cparams(("arbitrary",)),
    )(z, z, z)


def _na_window_start(i, rows, win_r):
    nkr = win_r + NA_QROWS
    first = jnp.clip(NA_QROWS * i - win_r // 2, 0, rows - win_r)
    return jnp.minimum(first, rows - nkr)


def _na_kernel(q_ref, k_ref, v_ref, kc_ref, vc_ref, bias_ref, o_ref, *, rows, win_r, dh):
    i = pl.program_id(2)
    nk = (win_r + NA_QROWS) * GRID_W
    start = pl.multiple_of(_na_window_start(i, rows, win_r) * GRID_W, GRID_W)
    scale = dh ** -0.5
    q = q_ref[...].astype(BF16)
    kw = k_ref[pl.ds(start, nk), :].astype(BF16)
    vw = v_ref[pl.ds(start, nk), :].astype(BF16)
    s_loc = _dot_nt(q, kw) * scale + bias_ref[0, 0]
    s_ctx = _dot_nt(q, kc_ref[...].astype(BF16)) * scale
    m = jnp.maximum(jnp.max(s_loc, axis=-1, keepdims=True), jnp.max(s_ctx, axis=-1, keepdims=True))
    e_loc = jnp.exp(s_loc - m)
    e_ctx = jnp.exp(s_ctx - m)
    den = jnp.sum(e_loc, axis=-1, keepdims=True) + jnp.sum(e_ctx, axis=-1, keepdims=True)
    o = (_dot(e_loc.astype(BF16), vw) + _dot(e_ctx.astype(BF16), vc_ref[...].astype(BF16))) / den
    o_ref[...] = o.astype(o_ref.dtype)


def _na_bias(rpb, rows):
    nh, nir, nic = rpb.shape
    win_r = (nir + 1) // 2
    win_c = (nic + 1) // 2
    wr = min(win_r, rows)
    nkr = wr + NA_QROWS
    nblk = rows // NA_QROWS
    col = np.arange(GRID_W)
    c0 = np.clip(col - win_c // 2, 0, GRID_W - win_c)
    col_ok = (col[None, :] >= c0[:, None]) & (col[None, :] < c0[:, None] + win_c)
    ic = np.clip(col[None, :] - col[:, None] + (win_c - 1), 0, 2 * win_c - 2)

    def row_pattern(i):
        start = min(int(np.clip(NA_QROWS * i - wr // 2, 0, rows - wr)), rows - nkr)
        r = NA_QROWS * i + np.arange(NA_QROWS)
        kr = start + np.arange(nkr)
        kstart = np.clip(r - wr // 2, 0, rows - wr)
        row_ok = (kr[None, :] >= kstart[:, None]) & (kr[None, :] < kstart[:, None] + wr)
        ir = np.clip(kr[None, :] - r[:, None] + (win_r - 1), 0, nir - 1)
        assert row_ok.sum(axis=1).min() == wr
        return row_ok, np.where(row_ok, ir, 0)

    for i in range(2, nblk - 1):
        assert all(np.array_equal(a, b) for a, b in zip(row_pattern(i), row_pattern(1)))

    sel_c = jnp.asarray((ic[None] == np.arange(nic)[:, None, None]).astype(np.float32))
    by_col = jnp.einsum('hic,cqk->hiqk', rpb.astype(F32), sel_c, precision=HI)
    tables = []
    for i in (0, 1, nblk - 1):
        row_ok, ir = row_pattern(i)
        ok = row_ok[:, None, :, None] & col_ok[None, :, None, :]
        sel_r = jnp.asarray((ir[None] == np.arange(nir)[:, None, None]).astype(np.float32))
        vals = jnp.einsum('hiqk,iaj->haqjk', by_col, sel_r, precision=HI)
        vals = jnp.where(jnp.asarray(ok)[None], vals, NEG_INF)
        tables.append(vals.reshape(nh, NA_QROWS * GRID_W, nkr * GRID_W))
    return jnp.stack(tables), wr


def _na_attn(z, kc, vc, bias, *, nbatch, seq, past, nh, row0, win_r, dcols):
    dh = dcols // nh
    rows = seq // GRID_W
    tq = NA_QROWS * GRID_W
    nblk = rows // NA_QROWS
    assert row0 % seq == 0 and rows % NA_QROWS == 0 and rows >= win_r + NA_QROWS
    kern = functools.partial(_na_kernel, rows=rows, win_r=win_r, dh=dh)
    qoff = row0 // tq
    boff = row0 // seq

    def cls(i):
        return jnp.where(i == 0, 0, jnp.where(i == nblk - 1, 2, 1))

    c_spec = pl.BlockSpec((past, dh), lambda b, h, i: (b, h))
    return pl.pallas_call(
        kern,
        grid=(nbatch, nh, nblk),
        in_specs=[
            pl.BlockSpec((tq, dh), lambda b, h, i: (qoff + b * nblk + i, nh + h)),
            pl.BlockSpec((seq, dh), lambda b, h, i: (boff + b, 2 * nh + h)),
            pl.BlockSpec((seq, dh), lambda b, h, i: (boff + b, 3 * nh + h)),
            c_spec, c_spec,
            pl.BlockSpec((1, 1) + bias.shape[2:], lambda b, h, i: (cls(i), h, 0, 0)),
        ],
        out_specs=pl.BlockSpec((tq, dh), lambda b, h, i: (b * nblk + i, h)),
        out_shape=jax.ShapeDtypeStruct((nbatch * seq, dcols), BF16),
        name="na_attn",
        compiler_params=_cparams(("arbitrary", "arbitrary", "arbitrary")),
    )(z, z, z, kc, vc, bias)


def _even_out_kernel(ys_ref, yb_ref, x_ref, m_ref, g_ref, wglu_ref, woa_ref, wob_ref, o_ref):
    y = jax.nn.gelu(ys_ref[...])
    gate = jax.nn.sigmoid(_dot(y.astype(BF16), wglu_ref[...]))
    ya = (y * gate).astype(BF16)
    o = _dot(ya, woa_ref[...]) + _dot(yb_ref[...], wob_ref[...])
    r = _rms(o) * g_ref[3:4, :]
    o_ref[...] = x_ref[...] + m_ref[0, 5:6, :] * r


def _even_out(ys, yb, x, mod_l, g_l, w_glu, w_out, *, gid, tm):
    n, d = x.shape
    da = ys.shape[1]
    db = yb.shape[1]
    assert da == db
    return pl.pallas_call(
        _even_out_kernel,
        grid=(n // tm,),
        in_specs=[
            pl.BlockSpec((tm, da), lambda i: (i, 0)),
            pl.BlockSpec((tm, db), lambda i: (i, 0)),
            pl.BlockSpec((tm, d), lambda i: (i, 0)),
            pl.BlockSpec((1, N_MOD, d), lambda i: (gid(i), 0, 0)),
            pl.BlockSpec(g_l.shape, lambda i: (0, 0)),
            _resident((da, da), lambda i: (0, 0)),
            _resident((da, d), lambda i: (0, 0)),
            _resident((db, d), lambda i: (1, 0)),
        ],
        out_specs=pl.BlockSpec((tm, d), lambda i: (i, 0)),
        out_shape=jax.ShapeDtypeStruct((n, d), F32),
        name="even_out",
        compiler_params=_cparams(("arbitrary",)),
    )(ys, yb, x, mod_l, g_l, w_glu, w_out, w_out)


def _odd_in_kernel(x_ref, m_ref, g_ref, wb_ref, wc_ref, wx_ref, b_ref, p_ref, h_ref):
    j = pl.program_id(1)

    @pl.when(j == 0)
    def _():
        h_ref[...] = _modnorm(x_ref[...], g_ref[2:3, :], m_ref[0, 3:4, :], m_ref[0, 4:5, :]).astype(BF16)

    h = h_ref[...]
    b_ref[...] = _dot(h, wb_ref[...])
    p_ref[...] = _dot(h, wc_ref[...]) * _dot(h, wx_ref[...])


def _odd_in(x, mod_l, g_l, w, *, gid, tm):
    n, d = x.shape
    dc = w.shape[1] // 3
    tn = _tile(dc, 512)
    nj = dc // tn
    blk = pl.BlockSpec((tm, tn), lambda i, j: (i, j))
    return pl.pallas_call(
        _odd_in_kernel,
        grid=(n // tm, nj),
        in_specs=[
            pl.BlockSpec((tm, d), lambda i, j: (i, 0)),
            pl.BlockSpec((1, N_MOD, d), lambda i, j: (gid(i), 0, 0)),
            pl.BlockSpec(g_l.shape, lambda i, j: (0, 0)),
            pl.BlockSpec((d, tn), lambda i, j: (0, j)),
            pl.BlockSpec((d, tn), lambda i, j: (0, j + nj)),
            pl.BlockSpec((d, tn), lambda i, j: (0, j + 2 * nj)),
        ],
        out_specs=[blk, blk],
        out_shape=[jax.ShapeDtypeStruct((n, dc), F32), jax.ShapeDtypeStruct((n, dc), F32)],
        scratch_shapes=[pltpu.VMEM((tm, d), BF16)],
        name="odd_in",
        compiler_params=_cparams(("arbitrary", "arbitrary")),
    )(x, mod_l, g_l, w, w, w)


def _odd_out_kernel(b_ref, p_ref, pp_ref, pn_ref, cw_ref, x_ref, m_ref, g_ref, w_ref, o_ref,
                    *, tm, ctx_rows, ctx_seq, lat_seq):
    i = pl.program_id(0)
    p = p_ref[...]
    row = lax.broadcasted_iota(jnp.int32, (tm, 1), 0)
    grow = i * tm + row
    is_ctx = grow < ctx_rows
    pos = jnp.where(is_ctx, grow % ctx_seq, (grow - ctx_rows) % lat_seq)
    last = jnp.where(is_ctx, ctx_seq - 1, lat_seq - 1)
    prev = pltpu.roll(p, 1, axis=0)
    prev = jnp.where(row == 0, pp_ref[SUBLANE - 1:SUBLANE, :], prev)
    prev = jnp.where(pos == 0, 0.0, prev)
    nxt = pltpu.roll(p, tm - 1, axis=0)
    nxt = jnp.where(row == tm - 1, pn_ref[0:1, :], nxt)
    nxt = jnp.where(pos == last, 0.0, nxt)
    y = cw_ref[0:1, :] * prev + cw_ref[1:2, :] * p + cw_ref[2:3, :] * nxt
    v = (b_ref[...] * y).astype(BF16)
    o = _dot(v, w_ref[...])
    r = _rms(o) * g_ref[3:4, :]
    o_ref[...] = x_ref[...] + m_ref[0, 5:6, :] * r


def _odd_out(b, p, conv_w, x, mod_l, g_l, w, *, gid, tm, ctx_rows, ctx_seq, lat_seq):
    n, d = x.shape
    dc = b.shape[1]
    assert conv_w.shape[0] == 3
    kern = functools.partial(_odd_out_kernel, tm=tm, ctx_rows=ctx_rows, ctx_seq=ctx_seq, lat_seq=lat_seq)
    tpb = tm // SUBLANE
    nb8 = n // SUBLANE
    return pl.pallas_call(
        kern,
        grid=(n // tm,),
        in_specs=[
            pl.BlockSpec((tm, dc), lambda i: (i, 0)),
            pl.BlockSpec((tm, dc), lambda i: (i, 0)),
            pl.BlockSpec((SUBLANE, dc), lambda i: (jnp.maximum(i * tpb - 1, 0), 0)),
            pl.BlockSpec((SUBLANE, dc), lambda i: (jnp.minimum((i + 1) * tpb, nb8 - 1), 0)),
            pl.BlockSpec(conv_w.shape, lambda i: (0, 0)),
            pl.BlockSpec((tm, d), lambda i: (i, 0)),
            pl.BlockSpec((1, N_MOD, d), lambda i: (gid(i), 0, 0)),
            pl.BlockSpec(g_l.shape, lambda i: (0, 0)),
            _resident((dc, d), lambda i: (0, 0)),
        ],
        out_specs=pl.BlockSpec((tm, d), lambda i: (i, 0)),
        out_shape=jax.ShapeDtypeStruct((n, d), F32),
        name="odd_out",
        compiler_params=_cparams(("arbitrary",)),
    )(b, p, p, p, conv_w, x, mod_l, g_l, w)


def kernel(x_prompt, x_sample, cache_k, cache_v, state_s5_re, state_s5_im, c, c_ctx, norm_g, w_mod, b_mod,
           w_ffn_in, w_ffn_out, w_in_even, w_out_even, s5_lam_re, s5_lam_im, s5_log_dt, s5_b_re, s5_b_im,
           s5_c_re, s5_c_im, s5_d, w_glu, na_rpb, w_in_conv, conv_w, w_out_conv):
    nb_c, seq_c, d = x_prompt.shape
    nb_l, seq_l, _ = x_sample.shape
    depth = norm_g.shape[0]
    past, nh = cache_k.shape[2], cache_k.shape[3]
    ngrp, p_len = state_s5_re.shape[3], state_s5_re.shape[4]
    h_len = s5_b_re.shape[-1]
    d_a = ngrp * h_len
    ctx_rows = nb_c * seq_c
    n = ctx_rows + nb_l * seq_l
    tm = _tile(math.gcd(ctx_rows, seq_l), 512)
    gb = LANE // h_len
    assert gb * h_len == LANE and ngrp % gb == 0 and 2 * p_len == LANE and 2 * d_a == d

    tm_c = _tile(tm, 256)

    def group_of_tile(rows_per_tile):
        def gid(i):
            r0 = i * rows_per_tile
            return jnp.where(r0 < ctx_rows, 0, 1 + jnp.maximum(r0 - ctx_rows, 0) // seq_l)
        return gid

    gid = group_of_tile(tm)
    gid_c = group_of_tile(tm_c)

    ng = 1 + nb_l
    ng8 = -(-ng // SUBLANE) * SUBLANE
    cvec = jnp.concatenate([c_ctx[None], c, jnp.zeros((ng8 - ng, d), F32)], axis=0)
    mod = _modulation(cvec, w_mod, b_mod).reshape(depth, ng8, N_MOD, d)

    x = jnp.concatenate([x_prompt.reshape(ctx_rows, d), x_sample.reshape(nb_l * seq_l, d)], axis=0)

    new_k, new_v, new_sre, new_sim = [], [], [], []
    for l in range(depth):
        g_l = norm_g[l]
        mod_l = mod[l]
        x = _ffn(x, mod_l, g_l, w_ffn_in[l, 0].astype(BF16), w_ffn_out[l, 0].astype(BF16),
                 mi=0, gi=0, gid=gid, tm=tm)
        if l % 2 == 0:
            e = l // 2
            z = _even_in(x, mod_l, g_l, w_in_even[e].astype(BF16), gid=gid, tm=tm)
            new_k.append(z[:ctx_rows, 2 * d_a:3 * d_a].reshape(nb_c, seq_c, nh, -1))
            new_v.append(z[:ctx_rows, 3 * d_a:].reshape(nb_c, seq_c, nh, -1))
            kst, kfull, kout, ar, ai = _s5_weights(s5_lam_re[e], s5_lam_im[e], s5_log_dt[e], s5_b_re[e],
                                                   s5_b_im[e], s5_c_re[e], s5_c_im[e], s5_d[e], gb)
            h0_c = jnp.zeros((nb_c, ngrp * 4 * p_len), F32)
            ys_c, hfin = _s5(_to_chunks(z[:ctx_rows, :d_a], nb_c, seq_c), kst, kfull, kout, ar, ai,
                             h0_c, nseq=nb_c)
            fin_re, fin_im = _cols_to_state(hfin, ngrp, p_len, gb)
            new_sre.append(fin_re)
            new_sim.append(fin_im)
            h0_l = _state_to_cols(state_s5_re[:, e], state_s5_im[:, e], gb)
            ys_l, _ = _s5(_to_chunks(z[ctx_rows:, :d_a], nb_l, seq_l), kst, kfull, kout, ar, ai,
                          h0_l, nseq=nb_l)
            ys = jnp.concatenate([_from_chunks(ys_c, nb_c, seq_c), _from_chunks(ys_l, nb_l, seq_l)], axis=0)
            yb_c = _ctx_attn(z, nbatch=nb_c, seq=seq_c, nh=nh, dcols=d_a)
            bias, win_r = _na_bias(na_rpb[e], seq_l // GRID_W)
            yb_l = _na_attn(z, cache_k[:, e].reshape(nb_l * past, -1),
                            cache_v[:, e].reshape(nb_l * past, -1), bias,
                            nbatch=nb_l, seq=seq_l, past=past, nh=nh, row0=ctx_rows, win_r=win_r, dcols=d_a)
            yb = jnp.concatenate([yb_c, yb_l], axis=0)
            x = _even_out(ys, yb, x, mod_l, g_l, w_glu[e].astype(BF16), w_out_even[e].astype(BF16),
                          gid=gid, tm=tm)
        else:
            o = l // 2
            b, p = _odd_in(x, mod_l, g_l, w_in_conv[o].astype(BF16), gid=gid, tm=tm)
            x = _odd_out(b, p, conv_w[o], x, mod_l, g_l, w_out_conv[o].astype(BF16), gid=gid_c, tm=tm_c,
                         ctx_rows=ctx_rows, ctx_seq=seq_c, lat_seq=seq_l)
        x = _ffn(x, mod_l, g_l, w_ffn_in[l, 1].astype(BF16), w_ffn_out[l, 1].astype(BF16),
                 mi=6, gi=4, gid=gid, tm=tm)

    y_prompt = x[:ctx_rows].reshape(nb_c, seq_c, d)
    y_sample = x[ctx_rows:].reshape(nb_l, seq_l, d)
    return (y_prompt, y_sample, jnp.stack(new_k, axis=1), jnp.stack(new_v, axis=1),
            jnp.stack(new_sre, axis=1), jnp.stack(new_sim, axis=1))
```

```python
import functools
import math

import numpy as np
import jax
import jax.numpy as jnp
from jax import lax
from jax.experimental import pallas as pl
from jax.experimental.pallas import tpu as pltpu

F32 = jnp.float32
BF16 = jnp.bfloat16

NORM_EPS = 1e-6
N_MOD = 9
GRID_W = 64
NEG_INF = -1e30
S5_CHUNK = 8
NA_QROWS = 4
LANE = 128
SUBLANE = 8
VMEM_LIMIT_BYTES = 56 * 1024 * 1024
HI = lax.Precision.HIGHEST


def _cparams(sem):
    return pltpu.CompilerParams(dimension_semantics=sem, vmem_limit_bytes=VMEM_LIMIT_BYTES)


def _tile(n, pref):
    t = min(n, pref)
    while n % t:
        t //= 2
    return t


def _resident(block_shape, index_map):
    return pl.BlockSpec(block_shape, index_map, pipeline_mode=pl.Buffered(1))


def _rms(x):
    return x * lax.rsqrt(jnp.mean(x * x, axis=-1, keepdims=True) + NORM_EPS)


def _modnorm(x, g, shift, scale):
    return (_rms(x) * g) * (1.0 + scale) + shift


def _dot(a, b):
    return jnp.dot(a, b, preferred_element_type=F32)


def _dot_nt(a, b):
    return lax.dot_general(a, b, (((1,), (1,)), ((), ())), preferred_element_type=F32)


def _mod_kernel(c_ref, w_ref, b_ref, o_ref):
    c = c_ref[...]
    s = (c * jax.nn.sigmoid(c)).astype(BF16)
    o_ref[0] = _dot(s, w_ref[0].astype(BF16)) + b_ref[0]


def _modulation(cvec, w_mod, b_mod):
    depth, d, nd = w_mod.shape
    ng = cvec.shape[0]
    tn = _tile(nd, 1024)
    return pl.pallas_call(
        _mod_kernel,
        grid=(depth, nd // tn),
        in_specs=[
            pl.BlockSpec((ng, d), lambda l, j: (0, 0)),
            pl.BlockSpec((1, d, tn), lambda l, j: (l, 0, j)),
            pl.BlockSpec((1, 1, tn), lambda l, j: (l, 0, j)),
        ],
        out_specs=pl.BlockSpec((1, ng, tn), lambda l, j: (l, 0, j)),
        out_shape=jax.ShapeDtypeStruct((depth, ng, nd), F32),
        name="modulation",
        compiler_params=_cparams(("arbitrary", "arbitrary")),
    )(cvec, w_mod, b_mod.reshape(depth, 1, nd))


def _ffn_kernel(x_ref, m_ref, g_ref, wg_ref, wu_ref, wo_ref, o_ref, h_ref, *, mi, gi, nj):
    j = pl.program_id(1)

    @pl.when(j == 0)
    def _():
        h_ref[...] = _modnorm(x_ref[...], g_ref[gi:gi + 1, :], m_ref[0, mi:mi + 1, :],
                              m_ref[0, mi + 1:mi + 2, :]).astype(BF16)
        o_ref[...] = jnp.zeros_like(o_ref)

    h = h_ref[...]
    a = _dot(h, wg_ref[...])
    u = _dot(h, wu_ref[...])
    hid = (a * jax.nn.sigmoid(a) * u).astype(BF16)
    o_ref[...] += _dot(hid, wo_ref[...])

    @pl.when(j == nj - 1)
    def _():
        r = _rms(o_ref[...]) * g_ref[gi + 1:gi + 2, :]
        o_ref[...] = x_ref[...] + 0.5 * m_ref[0, mi + 2:mi + 3, :] * r


def _ffn(x, mod_l, g_l, w_in, w_out, *, mi, gi, gid, tm, row0=0, nrows=None):
    d = x.shape[1]
    n = x.shape[0] if nrows is None else nrows
    t0 = row0 // tm
    dff = w_out.shape[0]
    tf = _tile(dff, 512)
    nj = dff // tf
    kern = functools.partial(_ffn_kernel, mi=mi, gi=gi, nj=nj)
    return pl.pallas_call(
        kern,
        grid=(n // tm, nj),
        in_specs=[
            pl.BlockSpec((tm, d), lambda i, j: (t0 + i, 0)),
            pl.BlockSpec((1, N_MOD, d), lambda i, j: (gid(t0 + i), 0, 0)),
            pl.BlockSpec(g_l.shape, lambda i, j: (0, 0)),
            pl.BlockSpec((d, tf), lambda i, j: (0, j)),
            pl.BlockSpec((d, tf), lambda i, j: (0, j + nj)),
            pl.BlockSpec((tf, d), lambda i, j: (j, 0)),
        ],
        out_specs=pl.BlockSpec((tm, d), lambda i, j: (i, 0)),
        out_shape=jax.ShapeDtypeStruct((n, d), F32),
        scratch_shapes=[pltpu.VMEM((tm, d), BF16)],
        name="ffn",
        compiler_params=_cparams(("arbitrary", "arbitrary")),
    )(x, mod_l, g_l, w_in, w_in, w_out)


def _even_in_kernel(x_ref, m_ref, g_ref, w_ref, z_ref, h_ref):
    @pl.when(pl.program_id(1) == 0)
    def _():
        h_ref[...] = _modnorm(x_ref[...], g_ref[2:3, :], m_ref[0, 3:4, :], m_ref[0, 4:5, :]).astype(BF16)

    z_ref[...] = _dot(h_ref[...], w_ref[...])


def _even_in(x, mod_l, g_l, w, *, gid, tm):
    n, d = x.shape
    nz = w.shape[1]
    tn = _tile(nz, 1024)
    return pl.pallas_call(
        _even_in_kernel,
        grid=(n // tm, nz // tn),
        in_specs=[
            pl.BlockSpec((tm, d), lambda i, j: (i, 0)),
            pl.BlockSpec((1, N_MOD, d), lambda i, j: (gid(i), 0, 0)),
            pl.BlockSpec(g_l.shape, lambda i, j: (0, 0)),
            pl.BlockSpec((d, tn), lambda i, j: (0, j)),
        ],
        out_specs=pl.BlockSpec((tm, tn), lambda i, j: (i, j)),
        out_shape=jax.ShapeDtypeStruct((n, nz), F32),
        scratch_shapes=[pltpu.VMEM((tm, d), BF16)],
        name="even_in",
        compiler_params=_cparams(("arbitrary", "arbitrary")),
    )(x, mod_l, g_l, w)


def _s5_kernel(u_ref, kst_ref, kfull_ref, kout_ref, ar_ref, ai_ref, h0_ref, y_ref, hfin_ref, s_ref,
               *, nseq, nchunk, rt):
    t_len, rows, _ = u_ref.shape
    w = s_ref.shape[2]
    w2 = w // 2

    def chunk_lhs(r0):
        return jnp.concatenate([u_ref[t, r0:r0 + rt, :] for t in range(t_len)], axis=-1)

    for r0 in range(0, rows, rt):
        res = _dot(chunk_lhs(r0), kst_ref[0])
        s_ref[r0 // SUBLANE:(r0 + rt) // SUBLANE] = res.reshape(rt // SUBLANE, SUBLANE, w)

    ar = ar_ref[...]
    ai = ai_ref[...]

    def step(hr, hi, sr, si):
        return ar * hr - ai * hi + sr, ar * hi + ai * hr + si

    if nseq % SUBLANE == 0:
        nb = nseq // SUBLANE
        d0 = (lax.broadcasted_iota(jnp.int32, (nb, SUBLANE, w2), 2) & (LANE // 2)) == 0
        hr0 = h0_ref[:, 0:w2].reshape(nb, SUBLANE, w2)
        hi0 = h0_ref[:, w2:w].reshape(nb, SUBLANE, w2)

        def body(k, carry):
            hr, hi = carry
            k1 = nchunk - 1 - k
            b0r = s_ref[pl.ds(k * nb, nb), :, 0:w2]
            b0i = s_ref[pl.ds(k * nb, nb), :, w2:w]
            b1r = s_ref[pl.ds(k1 * nb, nb), :, 0:w2]
            b1i = s_ref[pl.ds(k1 * nb, nb), :, w2:w]
            s_ref[pl.ds(k * nb, nb), :, 0:w2] = jnp.where(d0, hr, b0r)
            s_ref[pl.ds(k * nb, nb), :, w2:w] = jnp.where(d0, hi, b0i)
            s_ref[pl.ds(k1 * nb, nb), :, 0:w2] = jnp.where(d0, b1r, hr)
            s_ref[pl.ds(k1 * nb, nb), :, w2:w] = jnp.where(d0, b1i, hi)
            return step(hr, hi, jnp.where(d0, b0r, b1r), jnp.where(d0, b0i, b1i))

        assert nchunk % 2 == 0
        hr, hi = lax.fori_loop(0, nchunk, body, (hr0, hi0))
        hfin_ref[:, 0:w2] = hr.reshape(nseq, w2)
        hfin_ref[:, w2:w] = hi.reshape(nseq, w2)
    else:
        half = SUBLANE // 2
        assert nseq == half and nchunk % 4 == 0
        nblk = nchunk // 2
        d0 = (lax.broadcasted_iota(jnp.int32, (half, w2), 1) & (LANE // 2)) == 0
        lo = slice(0, half)
        up = slice(half, SUBLANE)

        def body(k, carry):
            hr, hi = carry
            k1 = nblk - 1 - k
            b0r = s_ref[k, :, 0:w2]
            b0i = s_ref[k, :, w2:w]
            b1r = s_ref[k1, :, 0:w2]
            b1i = s_ref[k1, :, w2:w]
            for fa, ba in ((lo, up), (up, lo)):
                s_ref[k, fa, 0:w2] = jnp.where(d0, hr, b0r[fa])
                s_ref[k, fa, w2:w] = jnp.where(d0, hi, b0i[fa])
                s_ref[k1, ba, 0:w2] = jnp.where(d0, b1r[ba], hr)
                s_ref[k1, ba, w2:w] = jnp.where(d0, b1i[ba], hi)
                hr, hi = step(hr, hi, jnp.where(d0, b0r[fa], b1r[ba]), jnp.where(d0, b0i[fa], b1i[ba]))
            return hr, hi

        hr, hi = lax.fori_loop(0, nblk, body, (h0_ref[:, 0:w2], h0_ref[:, w2:w]))
        hfin_ref[:, 0:w2] = hr
        hfin_ref[:, w2:w] = hi

    for r0 in range(0, rows, rt):
        hp = s_ref[r0 // SUBLANE:(r0 + rt) // SUBLANE].reshape(rt, w).astype(BF16)
        y = _dot(chunk_lhs(r0), kfull_ref[0]) + _dot(hp, kout_ref[0])
        for t in range(t_len):
            y_ref[t, r0:r0 + rt, :] = y[:, t * LANE:(t + 1) * LANE].astype(y_ref.dtype)


def _s5(u_t, kst, kfull, kout, ar, ai, h0, *, nseq):
    t_len, rows, _ = u_t.shape
    nblk, kdim, sdim = kst.shape
    nchunk = rows // nseq
    kern = functools.partial(_s5_kernel, nseq=nseq, nchunk=nchunk, rt=_tile(rows, 512))
    return pl.pallas_call(
        kern,
        grid=(nblk,),
        in_specs=[
            pl.BlockSpec((t_len, rows, LANE), lambda i: (0, 0, i)),
            _resident((1, kdim, sdim), lambda i: (i, 0, 0)),
            _resident((1, kdim, kdim), lambda i: (i, 0, 0)),
            _resident((1, sdim, kdim), lambda i: (i, 0, 0)),
            pl.BlockSpec((1, sdim // 2), lambda i: (0, i)),
            pl.BlockSpec((1, sdim // 2), lambda i: (0, i)),
            pl.BlockSpec((nseq, sdim), lambda i: (0, i)),
        ],
        out_specs=[pl.BlockSpec((t_len, rows, LANE), lambda i: (0, 0, i)),
                   pl.BlockSpec((nseq, sdim), lambda i: (0, i))],
        out_shape=[jax.ShapeDtypeStruct(u_t.shape, BF16),
                   jax.ShapeDtypeStruct((nseq, nblk * sdim), F32)],
        scratch_shapes=[pltpu.VMEM((rows // SUBLANE, SUBLANE, sdim), F32)],
        name="s5_seq%d" % nseq,
        compiler_params=_cparams(("arbitrary",)),
    )(u_t, kst, kfull, kout, ar, ai, h0)


def _s5_weights(lam_re, lam_im, log_dt, b_re, b_im, c_re, c_im, d_skip, gb):
    t_len = S5_CHUNK
    _, ngrp, p_len = lam_re.shape
    h_len = b_re.shape[-1]
    dt = jnp.exp(log_dt)[..., None]
    e = jnp.exp(lam_re * dt)
    lbr = e * jnp.cos(lam_im * dt)
    lbi = e * jnp.sin(lam_im * dt)
    den = lam_re * lam_re + lam_im * lam_im
    fr = ((lbr - 1.0) * lam_re + lbi * lam_im) / den
    fi = (lbi * lam_re - (lbr - 1.0) * lam_im) / den
    br = fr[..., None] * b_re - fi[..., None] * b_im
    bi = fr[..., None] * b_im + fi[..., None] * b_re
    pr = [jnp.ones_like(lbr)]
    pi = [jnp.zeros_like(lbr)]
    for _ in range(t_len):
        pr.append(pr[-1] * lbr - pi[-1] * lbi)
        pi.append(pr[-2] * lbi + pi[-1] * lbr)
    pwr = jnp.stack(pr)
    pwi = jnp.stack(pi)

    tt = np.arange(t_len)
    er = jnp.stack([pwr[t_len - 1 - tt, 0], pwr[tt, 1]], axis=1)
    ei = jnp.stack([pwi[t_len - 1 - tt, 0], pwi[tt, 1]], axis=1)
    ks_re = er[..., None] * br[None] - ei[..., None] * bi[None]
    ks_im = er[..., None] * bi[None] + ei[..., None] * br[None]

    nblk = ngrp // gb
    kdim = t_len * gb * h_len
    sdim = gb * 2 * p_len
    grp_of_k = (jnp.arange(kdim) // h_len) % gb
    grp_of_s = jnp.arange(sdim) // (2 * p_len)

    def st_cols(x):
        x = jnp.transpose(x.reshape(t_len, 2, nblk, gb, p_len, h_len), (2, 0, 3, 5, 1, 4))
        x = jnp.tile(x.reshape(nblk, kdim, 2 * p_len), (1, 1, gb))
        return jnp.where(grp_of_k[:, None] == grp_of_s[None, :], x, 0.0)

    kst = jnp.concatenate([st_cols(ks_re), st_cols(ks_im)], axis=-1)

    qr = jnp.stack([pwr[tt + 1, 0], pwr[t_len - tt, 1]], axis=1)
    qi = jnp.stack([pwi[tt + 1, 0], pwi[t_len - tt, 1]], axis=1)
    cqr = c_re[None] * qr[:, :, :, None, :] - c_im[None] * qi[:, :, :, None, :]
    cqi = c_re[None] * qi[:, :, :, None, :] + c_im[None] * qr[:, :, :, None, :]

    def out_rows(x):
        x = jnp.transpose(x.reshape(t_len, 2, nblk, gb, h_len, p_len), (2, 1, 5, 0, 3, 4))
        x = jnp.tile(x.reshape(nblk, 2 * p_len, kdim), (1, gb, 1))
        return jnp.where(grp_of_s[:, None] == grp_of_k[None, :], x, 0.0)

    kout = jnp.concatenate([out_rows(cqr), out_rows(-cqi)], axis=1)

    cpr = c_re[None] * pwr[:t_len, :, :, None, :] - c_im[None] * pwi[:t_len, :, :, None, :]
    cpi = c_re[None] * pwi[:t_len, :, :, None, :] + c_im[None] * pwr[:t_len, :, :, None, :]
    m = (jnp.einsum('kdghp,dgpj->kdghj', cpr, br, precision=HI)
         - jnp.einsum('kdghp,dgpj->kdghj', cpi, bi, precision=HI))
    mz = jnp.concatenate([m, jnp.zeros_like(m[:1])], axis=0)
    lag = tt[None, :] - tt[:, None]
    f_idx = np.where(lag >= 0, lag, t_len)
    b_idx = np.where(lag <= 0, -lag, t_len)
    full = mz[f_idx, 0] + mz[b_idx, 1]
    eye_t = jnp.asarray(np.eye(t_len, dtype=np.float32))
    eye_h = jnp.asarray(np.eye(h_len, dtype=np.float32))
    full = full + (eye_t[:, :, None, None, None] * d_skip[None, None, :, :, None] * eye_h[None, None, None])
    full = jnp.transpose(full.reshape(t_len, t_len, nblk, gb, h_len, h_len), (2, 0, 3, 5, 1, 4))
    full = full.reshape(nblk, kdim, t_len * h_len)
    kk = np.arange(kdim)
    th_of_k = (kk // (gb * h_len)) * h_len + kk % h_len
    spread = jnp.asarray((np.arange(t_len * h_len)[:, None] == th_of_k[None, :]).astype(np.float32))
    full = jnp.einsum('nrk,kc->nrc', full, spread, precision=HI)
    kfull = jnp.where(grp_of_k[:, None] == grp_of_k[None, :], full, 0.0)

    ar = jnp.transpose(pwr[t_len], (1, 0, 2)).reshape(1, ngrp * 2 * p_len)
    ai = jnp.transpose(pwi[t_len], (1, 0, 2)).reshape(1, ngrp * 2 * p_len)
    return kst.astype(BF16), kfull.astype(BF16), kout.astype(BF16), ar, ai


def _state_to_cols(s_re, s_im, gb):
    b, _, ngrp, p_len = s_re.shape
    def blk(x):
        return jnp.transpose(x, (0, 2, 1, 3)).reshape(b, ngrp // gb, 1, gb * 2 * p_len)
    return jnp.concatenate([blk(s_re), blk(s_im)], axis=2).reshape(b, ngrp * 4 * p_len)


def _cols_to_state(cols, ngrp, p_len, gb):
    b = cols.shape[0]
    x = cols.reshape(b, ngrp // gb, 2, gb, 2, p_len)
    x = jnp.transpose(x, (2, 0, 4, 1, 3, 5)).reshape(2, b, 2, ngrp, p_len)
    return x[0], x[1]


def _to_chunks(u, nseq, seq_len):
    nchunk = seq_len // S5_CHUNK
    x = u.reshape(nseq, nchunk, S5_CHUNK, u.shape[-1])
    return jnp.transpose(x, (2, 1, 0, 3)).reshape(S5_CHUNK, nchunk * nseq, u.shape[-1]).astype(BF16)


def _from_chunks(y, nseq, seq_len):
    nchunk = seq_len // S5_CHUNK
    x = y.reshape(S5_CHUNK, nchunk, nseq, y.shape[-1])
    return jnp.transpose(x, (2, 1, 0, 3)).reshape(nseq * seq_len, y.shape[-1])


def _ctx_attn_kernel(q_ref, k_ref, v_ref, o_ref, *, nh, dh):
    scale = dh ** -0.5
    for h in range(nh):
        sl = slice(h * dh, (h + 1) * dh)
        s = _dot_nt(q_ref[:, sl].astype(BF16), k_ref[:, sl].astype(BF16)) * scale
        e = jnp.exp(s - jnp.max(s, axis=-1, keepdims=True))
        o = _dot(e.astype(BF16), v_ref[:, sl].astype(BF16)) / jnp.sum(e, axis=-1, keepdims=True)
        o_ref[:, sl] = o.astype(o_ref.dtype)


def _ctx_attn(z, *, nbatch, seq, nh, dcols):
    kern = functools.partial(_ctx_attn_kernel, nh=nh, dh=dcols // nh)
    return pl.pallas_call(
        kern,
        grid=(nbatch,),
        in_specs=[pl.BlockSpec((seq, dcols), lambda b, c=c: (b, c)) for c in (1, 2, 3)],
        out_specs=pl.BlockSpec((seq, dcols), lambda b: (b, 0)),
        out_shape=jax.ShapeDtypeStruct((nbatch * seq, dcols), BF16),
        name="ctx_attn",
        compiler_params=_cparams(("arbitrary",)),
    )(z, z, z)


def _na_window_start(i, rows, win_r):
    nkr = win_r + NA_QROWS
    first = jnp.clip(NA_QROWS * i - win_r // 2, 0, rows - win_r)
    return jnp.minimum(first, rows - nkr)


def _na_kernel(q_ref, k_ref, v_ref, kc_ref, vc_ref, bias_ref, o_ref, kb_ref, vb_ref, *, rows, win_r, dh):
    i = pl.program_id(2)

    @pl.when(i == 0)
    def _():
        kb_ref[...] = k_ref[...].astype(BF16)
        vb_ref[...] = v_ref[...].astype(BF16)

    nk = (win_r + NA_QROWS) * GRID_W
    start = pl.multiple_of(_na_window_start(i, rows, win_r) * GRID_W, GRID_W)
    q = (q_ref[...] * (dh ** -0.5)).astype(BF16)
    s_loc = _dot_nt(q, kb_ref[pl.ds(start, nk), :]) + bias_ref[0, 0]
    s_ctx = _dot_nt(q, kc_ref[...])
    m = jnp.maximum(jnp.max(s_loc, axis=-1, keepdims=True), jnp.max(s_ctx, axis=-1, keepdims=True))
    e_loc = jnp.exp(s_loc - m)
    e_ctx = jnp.exp(s_ctx - m)
    den = jnp.sum(e_loc, axis=-1, keepdims=True) + jnp.sum(e_ctx, axis=-1, keepdims=True)
    o = (_dot(e_loc.astype(BF16), vb_ref[pl.ds(start, nk), :]) + _dot(e_ctx.astype(BF16), vc_ref[...])) / den
    o_ref[...] = o.astype(o_ref.dtype)


def _na_bias(rpb, rows):
    nh, nir, nic = rpb.shape
    win_r = (nir + 1) // 2
    win_c = (nic + 1) // 2
    wr = min(win_r, rows)
    nkr = wr + NA_QROWS
    nblk = rows // NA_QROWS
    col = np.arange(GRID_W)
    c0 = np.clip(col - win_c // 2, 0, GRID_W - win_c)
    col_ok = (col[None, :] >= c0[:, None]) & (col[None, :] < c0[:, None] + win_c)
    ic = np.clip(col[None, :] - col[:, None] + (win_c - 1), 0, 2 * win_c - 2)

    def row_pattern(i):
        start = min(int(np.clip(NA_QROWS * i - wr // 2, 0, rows - wr)), rows - nkr)
        r = NA_QROWS * i + np.arange(NA_QROWS)
        kr = start + np.arange(nkr)
        kstart = np.clip(r - wr // 2, 0, rows - wr)
        row_ok = (kr[None, :] >= kstart[:, None]) & (kr[None, :] < kstart[:, None] + wr)
        ir = np.clip(kr[None, :] - r[:, None] + (win_r - 1), 0, nir - 1)
        assert row_ok.sum(axis=1).min() == wr
        return row_ok, np.where(row_ok, ir, 0)

    for i in range(2, nblk - 1):
        assert all(np.array_equal(a, b) for a, b in zip(row_pattern(i), row_pattern(1)))

    sel_c = jnp.asarray((ic[None] == np.arange(nic)[:, None, None]).astype(np.float32))
    by_col = jnp.einsum('hic,cqk->hiqk', rpb.astype(F32), sel_c, precision=HI)
    tables = []
    for i in (0, 1, nblk - 1):
        row_ok, ir = row_pattern(i)
        ok = row_ok[:, None, :, None] & col_ok[None, :, None, :]
        sel_r = jnp.asarray((ir[None] == np.arange(nir)[:, None, None]).astype(np.float32))
        vals = jnp.einsum('hiqk,iaj->haqjk', by_col, sel_r, precision=HI)
        vals = jnp.where(jnp.asarray(ok)[None], vals, NEG_INF)
        tables.append(vals.reshape(nh, NA_QROWS * GRID_W, nkr * GRID_W))
    return jnp.stack(tables), wr


def _na_attn(z, kc, vc, bias, *, nbatch, seq, past, nh, row0, win_r, dcols):
    dh = dcols // nh
    rows = seq // GRID_W
    tq = NA_QROWS * GRID_W
    nblk = rows // NA_QROWS
    assert row0 % seq == 0 and rows % NA_QROWS == 0 and rows >= win_r + NA_QROWS
    kern = functools.partial(_na_kernel, rows=rows, win_r=win_r, dh=dh)
    qoff = row0 // tq
    boff = row0 // seq

    def cls(i):
        return jnp.where(i == 0, 0, jnp.where(i == nblk - 1, 2, 1))

    c_spec = pl.BlockSpec((past, dh), lambda b, h, i: (b, h))
    return pl.pallas_call(
        kern,
        grid=(nbatch, nh, nblk),
        in_specs=[
            pl.BlockSpec((tq, dh), lambda b, h, i: (qoff + b * nblk + i, nh + h)),
            pl.BlockSpec((seq, dh), lambda b, h, i: (boff + b, 2 * nh + h)),
            pl.BlockSpec((seq, dh), lambda b, h, i: (boff + b, 3 * nh + h)),
            c_spec, c_spec,
            pl.BlockSpec((1, 1) + bias.shape[2:], lambda b, h, i: (cls(i), h, 0, 0)),
        ],
        out_specs=pl.BlockSpec((tq, dh), lambda b, h, i: (b * nblk + i, h)),
        out_shape=jax.ShapeDtypeStruct((nbatch * seq, dcols), BF16),
        scratch_shapes=[pltpu.VMEM((seq, dh), BF16), pltpu.VMEM((seq, dh), BF16)],
        name="na_attn",
        compiler_params=_cparams(("arbitrary", "arbitrary", "arbitrary")),
    )(z, z, z, kc, vc, bias)


def _even_out_kernel(ys_ref, yb_ref, x_ref, m_ref, g_ref, wglu_ref, woa_ref, wob_ref, o_ref):
    y = jax.nn.gelu(ys_ref[...])
    gate = jax.nn.sigmoid(_dot(y.astype(BF16), wglu_ref[...]))
    ya = (y * gate).astype(BF16)
    o = _dot(ya, woa_ref[...]) + _dot(yb_ref[...], wob_ref[...])
    r = _rms(o) * g_ref[3:4, :]
    o_ref[...] = x_ref[...] + m_ref[0, 5:6, :] * r


def _even_out(ys, yb, x, mod_l, g_l, w_glu, w_out, *, gid, tm):
    n, d = x.shape
    da = ys.shape[1]
    db = yb.shape[1]
    assert da == db
    return pl.pallas_call(
        _even_out_kernel,
        grid=(n // tm,),
        in_specs=[
            pl.BlockSpec((tm, da), lambda i: (i, 0)),
            pl.BlockSpec((tm, db), lambda i: (i, 0)),
            pl.BlockSpec((tm, d), lambda i: (i, 0)),
            pl.BlockSpec((1, N_MOD, d), lambda i: (gid(i), 0, 0)),
            pl.BlockSpec(g_l.shape, lambda i: (0, 0)),
            _resident((da, da), lambda i: (0, 0)),
            _resident((da, d), lambda i: (0, 0)),
            _resident((db, d), lambda i: (1, 0)),
        ],
        out_specs=pl.BlockSpec((tm, d), lambda i: (i, 0)),
        out_shape=jax.ShapeDtypeStruct((n, d), F32),
        name="even_out",
        compiler_params=_cparams(("arbitrary",)),
    )(ys, yb, x, mod_l, g_l, w_glu, w_out, w_out)


def _odd_in_kernel(x_ref, m_ref, g_ref, wb_ref, wc_ref, wx_ref, b_ref, p_ref, h_ref):
    j = pl.program_id(1)

    @pl.when(j == 0)
    def _():
        h_ref[...] = _modnorm(x_ref[...], g_ref[2:3, :], m_ref[0, 3:4, :], m_ref[0, 4:5, :]).astype(BF16)

    h = h_ref[...]
    b_ref[...] = _dot(h, wb_ref[...])
    p_ref[...] = _dot(h, wc_ref[...]) * _dot(h, wx_ref[...])


def _odd_in(x, mod_l, g_l, w, *, gid, tm):
    n, d = x.shape
    dc = w.shape[1] // 3
    tn = _tile(dc, 512)
    nj = dc // tn
    blk = pl.BlockSpec((tm, tn), lambda i, j: (i, j))
    return pl.pallas_call(
        _odd_in_kernel,
        grid=(n // tm, nj),
        in_specs=[
            pl.BlockSpec((tm, d), lambda i, j: (i, 0)),
            pl.BlockSpec((1, N_MOD, d), lambda i, j: (gid(i), 0, 0)),
            pl.BlockSpec(g_l.shape, lambda i, j: (0, 0)),
            pl.BlockSpec((d, tn), lambda i, j: (0, j)),
            pl.BlockSpec((d, tn), lambda i, j: (0, j + nj)),
            pl.BlockSpec((d, tn), lambda i, j: (0, j + 2 * nj)),
        ],
        out_specs=[blk, blk],
        out_shape=[jax.ShapeDtypeStruct((n, dc), F32), jax.ShapeDtypeStruct((n, dc), F32)],
        scratch_shapes=[pltpu.VMEM((tm, d), BF16)],
        name="odd_in",
        compiler_params=_cparams(("arbitrary", "arbitrary")),
    )(x, mod_l, g_l, w, w, w)


def _odd_out_kernel(b_ref, p_ref, pp_ref, pn_ref, cw_ref, x_ref, m_ref, g_ref, w_ref, o_ref,
                    *, tm, ctx_rows, ctx_seq, lat_seq):
    i = pl.program_id(0)
    p = p_ref[...]
    row = lax.broadcasted_iota(jnp.int32, (tm, 1), 0)
    grow = i * tm + row
    is_ctx = grow < ctx_rows
    pos = jnp.where(is_ctx, grow % ctx_seq, (grow - ctx_rows) % lat_seq)
    last = jnp.where(is_ctx, ctx_seq - 1, lat_seq - 1)
    prev = pltpu.roll(p, 1, axis=0)
    prev = jnp.where(row == 0, pp_ref[SUBLANE - 1:SUBLANE, :], prev)
    prev = jnp.where(pos == 0, 0.0, prev)
    nxt = pltpu.roll(p, tm - 1, axis=0)
    nxt = jnp.where(row == tm - 1, pn_ref[0:1, :], nxt)
    nxt = jnp.where(pos == last, 0.0, nxt)
    y = cw_ref[0:1, :] * prev + cw_ref[1:2, :] * p + cw_ref[2:3, :] * nxt
    v = (b_ref[...] * y).astype(BF16)
    o = _dot(v, w_ref[...])
    r = _rms(o) * g_ref[3:4, :]
    o_ref[...] = x_ref[...] + m_ref[0, 5:6, :] * r


def _odd_out(b, p, conv_w, x, mod_l, g_l, w, *, gid, tm, ctx_rows, ctx_seq, lat_seq):
    n, d = x.shape
    dc = b.shape[1]
    assert conv_w.shape[0] == 3
    kern = functools.partial(_odd_out_kernel, tm=tm, ctx_rows=ctx_rows, ctx_seq=ctx_seq, lat_seq=lat_seq)
    tpb = tm // SUBLANE
    nb8 = n // SUBLANE
    return pl.pallas_call(
        kern,
        grid=(n // tm,),
        in_specs=[
            pl.BlockSpec((tm, dc), lambda i: (i, 0)),
            pl.BlockSpec((tm, dc), lambda i: (i, 0)),
            pl.BlockSpec((SUBLANE, dc), lambda i: (jnp.maximum(i * tpb - 1, 0), 0)),
            pl.BlockSpec((SUBLANE, dc), lambda i: (jnp.minimum((i + 1) * tpb, nb8 - 1), 0)),
            pl.BlockSpec(conv_w.shape, lambda i: (0, 0)),
            pl.BlockSpec((tm, d), lambda i: (i, 0)),
            pl.BlockSpec((1, N_MOD, d), lambda i: (gid(i), 0, 0)),
            pl.BlockSpec(g_l.shape, lambda i: (0, 0)),
            _resident((dc, d), lambda i: (0, 0)),
        ],
        out_specs=pl.BlockSpec((tm, d), lambda i: (i, 0)),
        out_shape=jax.ShapeDtypeStruct((n, d), F32),
        name="odd_out",
        compiler_params=_cparams(("arbitrary",)),
    )(b, p, p, p, conv_w, x, mod_l, g_l, w)


def kernel(x_prompt, x_sample, cache_k, cache_v, state_s5_re, state_s5_im, c, c_ctx, norm_g, w_mod, b_mod,
           w_ffn_in, w_ffn_out, w_in_even, w_out_even, s5_lam_re, s5_lam_im, s5_log_dt, s5_b_re, s5_b_im,
           s5_c_re, s5_c_im, s5_d, w_glu, na_rpb, w_in_conv, conv_w, w_out_conv):
    nb_c, seq_c, d = x_prompt.shape
    nb_l, seq_l, _ = x_sample.shape
    depth = norm_g.shape[0]
    past, nh = cache_k.shape[2], cache_k.shape[3]
    ngrp, p_len = state_s5_re.shape[3], state_s5_re.shape[4]
    h_len = s5_b_re.shape[-1]
    d_a = ngrp * h_len
    ctx_rows = nb_c * seq_c
    n = ctx_rows + nb_l * seq_l
    tm_w = _tile(math.gcd(ctx_rows, seq_l), 1024)
    tm = _tile(tm_w, 512)
    tm_c = _tile(tm, 256)
    gb = LANE // h_len
    assert gb * h_len == LANE and ngrp % gb == 0 and 2 * p_len == LANE and 2 * d_a == d

    def group_of_tile(rows_per_tile):
        def gid(i):
            r0 = i * rows_per_tile
            return jnp.where(r0 < ctx_rows, 0, 1 + jnp.maximum(r0 - ctx_rows, 0) // seq_l)
        return gid

    gid = group_of_tile(tm)
    gid_c = group_of_tile(tm_c)
    gid_w = group_of_tile(tm_w)

    ng = 1 + nb_l
    ng8 = -(-ng // SUBLANE) * SUBLANE
    cvec = jnp.concatenate([c_ctx[None], c, jnp.zeros((ng8 - ng, d), F32)], axis=0)
    mod = _modulation(cvec, w_mod, b_mod).reshape(depth, ng8, N_MOD, d)

    x = jnp.concatenate([x_prompt.reshape(ctx_rows, d), x_sample.reshape(nb_l * seq_l, d)], axis=0)

    new_k, new_v, new_sre, new_sim = [], [], [], []
    for l in range(depth):
        g_l = norm_g[l]
        mod_l = mod[l]
        x = _ffn(x, mod_l, g_l, w_ffn_in[l, 0].astype(BF16), w_ffn_out[l, 0].astype(BF16),
                 mi=0, gi=0, gid=gid, tm=tm)
        if l % 2 == 0:
            e = l // 2
            z = _even_in(x, mod_l, g_l, w_in_even[e].astype(BF16), gid=gid_w, tm=tm_w)
            new_k.append(z[:ctx_rows, 2 * d_a:3 * d_a].reshape(nb_c, seq_c, nh, -1))
            new_v.append(z[:ctx_rows, 3 * d_a:].reshape(nb_c, seq_c, nh, -1))
            kst, kfull, kout, ar, ai = _s5_weights(s5_lam_re[e], s5_lam_im[e], s5_log_dt[e], s5_b_re[e],
                                                   s5_b_im[e], s5_c_re[e], s5_c_im[e], s5_d[e], gb)
            h0_c = jnp.zeros((nb_c, ngrp * 4 * p_len), F32)
            ys_c, hfin = _s5(_to_chunks(z[:ctx_rows, :d_a], nb_c, seq_c), kst, kfull, kout, ar, ai,
                             h0_c, nseq=nb_c)
            fin_re, fin_im = _cols_to_state(hfin, ngrp, p_len, gb)
            new_sre.append(fin_re)
            new_sim.append(fin_im)
            h0_l = _state_to_cols(state_s5_re[:, e], state_s5_im[:, e], gb)
            ys_l, _ = _s5(_to_chunks(z[ctx_rows:, :d_a], nb_l, seq_l), kst, kfull, kout, ar, ai,
                          h0_l, nseq=nb_l)
            ys = jnp.concatenate([_from_chunks(ys_c, nb_c, seq_c), _from_chunks(ys_l, nb_l, seq_l)], axis=0)
            yb_c = _ctx_attn(z, nbatch=nb_c, seq=seq_c, nh=nh, dcols=d_a)
            bias, win_r = _na_bias(na_rpb[e], seq_l // GRID_W)
            yb_l = _na_attn(z, cache_k[:, e].reshape(nb_l * past, -1).astype(BF16),
                            cache_v[:, e].reshape(nb_l * past, -1).astype(BF16), bias,
                            nbatch=nb_l, seq=seq_l, past=past, nh=nh, row0=ctx_rows, win_r=win_r, dcols=d_a)
            yb = jnp.concatenate([yb_c, yb_l], axis=0)
            x = _even_out(ys, yb, x, mod_l, g_l, w_glu[e].astype(BF16), w_out_even[e].astype(BF16),
                          gid=gid, tm=tm)
        else:
            o = l // 2
            b, p = _odd_in(x, mod_l, g_l, w_in_conv[o].astype(BF16), gid=gid_w, tm=tm_w)
            x = _odd_out(b, p, conv_w[o], x, mod_l, g_l, w_out_conv[o].astype(BF16), gid=gid_c, tm=tm_c,
                         ctx_rows=ctx_rows, ctx_seq=seq_c, lat_seq=seq_l)
        ffn2 = functools.partial(_ffn, x, mod_l, g_l, w_ffn_in[l, 1].astype(BF16), w_ffn_out[l, 1].astype(BF16),
                                 mi=6, gi=4, gid=gid, tm=tm)
        if l < depth - 1:
            x = ffn2()
        else:
            y_prompt = ffn2(row0=0, nrows=ctx_rows).reshape(nb_c, seq_c, d)
            y_sample = ffn2(row0=ctx_rows, nrows=n - ctx_rows).reshape(nb_l, seq_l, d)

    return (y_prompt, y_sample, jnp.stack(new_k, axis=1), jnp.stack(new_v, axis=1),
            jnp.stack(new_sre, axis=1), jnp.stack(new_sim, axis=1))
```

```python
import functools
import math

import numpy as np
import jax
import jax.numpy as jnp
from jax import lax
from jax.experimental import pallas as pl
from jax.experimental.pallas import tpu as pltpu

F32 = jnp.float32
BF16 = jnp.bfloat16

NORM_EPS = 1e-6
N_MOD = 9
GRID_W = 64
NEG_INF = -1e30
S5_CHUNK = 8
NA_QROWS = 4
NA_HEADS = 2
LANE = 128
SUBLANE = 8
VMEM_LIMIT_BYTES = 56 * 1024 * 1024
HI = lax.Precision.HIGHEST


def _cparams(sem):
    return pltpu.CompilerParams(dimension_semantics=sem, vmem_limit_bytes=VMEM_LIMIT_BYTES)


def _tile(n, pref):
    t = min(n, pref)
    while n % t:
        t //= 2
    return t


def _resident(block_shape, index_map):
    return pl.BlockSpec(block_shape, index_map, pipeline_mode=pl.Buffered(1))


def _rms(x):
    return x * lax.rsqrt(jnp.mean(x * x, axis=-1, keepdims=True) + NORM_EPS)


def _modnorm(x, g, shift, scale):
    return (_rms(x) * g) * (1.0 + scale) + shift


def _dot(a, b):
    return jnp.dot(a, b, preferred_element_type=F32)


def _dot_nt(a, b):
    return lax.dot_general(a, b, (((1,), (1,)), ((), ())), preferred_element_type=F32)


def _mod_kernel(c_ref, w_ref, b_ref, o_ref):
    c = c_ref[...]
    s = (c * jax.nn.sigmoid(c)).astype(BF16)
    o_ref[0] = _dot(s, w_ref[0].astype(BF16)) + b_ref[0]


def _modulation(cvec, w_mod, b_mod):
    depth, d, nd = w_mod.shape
    ng = cvec.shape[0]
    tn = _tile(nd, 1024)
    return pl.pallas_call(
        _mod_kernel,
        grid=(depth, nd // tn),
        in_specs=[
            pl.BlockSpec((ng, d), lambda l, j: (0, 0)),
            pl.BlockSpec((1, d, tn), lambda l, j: (l, 0, j)),
            pl.BlockSpec((1, 1, tn), lambda l, j: (l, 0, j)),
        ],
        out_specs=pl.BlockSpec((1, ng, tn), lambda l, j: (l, 0, j)),
        out_shape=jax.ShapeDtypeStruct((depth, ng, nd), F32),
        name="modulation",
        compiler_params=_cparams(("arbitrary", "arbitrary")),
    )(cvec, w_mod, b_mod.reshape(depth, 1, nd))


def _ffn_kernel(x_ref, m_ref, g_ref, wg_ref, wu_ref, wo_ref, o_ref, h_ref, *, mi, gi, nj):
    j = pl.program_id(1)

    def step(first, last):
        if first:
            h = _modnorm(x_ref[...], g_ref[gi:gi + 1, :], m_ref[0, mi:mi + 1, :],
                         m_ref[0, mi + 1:mi + 2, :]).astype(BF16)
            h_ref[...] = h
        else:
            h = h_ref[...]
        a = _dot(h, wg_ref[0, 0])
        u = _dot(h, wu_ref[0, 0])
        hid = (a * jax.nn.sigmoid(a) * u).astype(BF16)
        acc = _dot(hid, wo_ref[0, 0])
        if not first:
            acc = o_ref[...] + acc
        if last:
            r = _rms(acc) * g_ref[gi + 1:gi + 2, :]
            acc = x_ref[...] + 0.5 * m_ref[0, mi + 2:mi + 3, :] * r
        o_ref[...] = acc

    if nj == 1:
        step(True, True)
        return
    pl.when(j == 0)(functools.partial(step, True, False))
    if nj > 2:
        pl.when(jnp.logical_and(j > 0, j < nj - 1))(functools.partial(step, False, False))
    pl.when(j == nj - 1)(functools.partial(step, False, True))


def _ffn(x, mod_l, g_l, w_in, w_out, *, l, k, mi, gi, gid, tm, row0=0, nrows=None):
    d = x.shape[1]
    n = x.shape[0] if nrows is None else nrows
    t0 = row0 // tm
    dff = w_out.shape[2]
    tf = _tile(dff, 512)
    nj = dff // tf
    kern = functools.partial(_ffn_kernel, mi=mi, gi=gi, nj=nj)
    return pl.pallas_call(
        kern,
        grid=(n // tm, nj),
        in_specs=[
            pl.BlockSpec((tm, d), lambda i, j: (t0 + i, 0)),
            pl.BlockSpec((1, N_MOD, d), lambda i, j: (gid(t0 + i), 0, 0)),
            pl.BlockSpec(g_l.shape, lambda i, j: (0, 0)),
            pl.BlockSpec((1, 1, d, tf), lambda i, j: (l, k, 0, j)),
            pl.BlockSpec((1, 1, d, tf), lambda i, j: (l, k, 0, j + nj)),
            pl.BlockSpec((1, 1, tf, d), lambda i, j: (l, k, j, 0)),
        ],
        out_specs=pl.BlockSpec((tm, d), lambda i, j: (i, 0)),
        out_shape=jax.ShapeDtypeStruct((n, d), F32),
        scratch_shapes=[pltpu.VMEM((tm, d), BF16)],
        name="ffn",
        compiler_params=_cparams(("arbitrary", "arbitrary")),
    )(x, mod_l, g_l, w_in, w_in, w_out)


def _mixer_in_h(first, x_ref, m_ref, g_ref, h_ref):
    if first:
        h = _modnorm(x_ref[...], g_ref[2:3, :], m_ref[0, 3:4, :], m_ref[0, 4:5, :]).astype(BF16)
        h_ref[...] = h
        return h
    return h_ref[...]


def _even_in_kernel(x_ref, m_ref, g_ref, w_ref, z_ref, h_ref):
    def step(first):
        z_ref[...] = _dot(_mixer_in_h(first, x_ref, m_ref, g_ref, h_ref), w_ref[...])

    j = pl.program_id(1)
    pl.when(j == 0)(functools.partial(step, True))
    pl.when(j > 0)(functools.partial(step, False))


def _even_in(x, mod_l, g_l, w, *, gid, tm):
    n, d = x.shape
    nz = w.shape[1]
    tn = _tile(nz, 1024)
    return pl.pallas_call(
        _even_in_kernel,
        grid=(n // tm, nz // tn),
        in_specs=[
            pl.BlockSpec((tm, d), lambda i, j: (i, 0)),
            pl.BlockSpec((1, N_MOD, d), lambda i, j: (gid(i), 0, 0)),
            pl.BlockSpec(g_l.shape, lambda i, j: (0, 0)),
            pl.BlockSpec((d, tn), lambda i, j: (0, j)),
        ],
        out_specs=pl.BlockSpec((tm, tn), lambda i, j: (i, j)),
        out_shape=jax.ShapeDtypeStruct((n, nz), F32),
        scratch_shapes=[pltpu.VMEM((tm, d), BF16)],
        name="even_in",
        compiler_params=_cparams(("arbitrary", "arbitrary")),
    )(x, mod_l, g_l, w)


def _s5_kernel(u_ref, kst_ref, kfull_ref, kout_ref, ar_ref, ai_ref, h0_ref, y_ref, hfin_ref, s_ref,
               *, nseq, nchunk, rt):
    t_len, rows, _ = u_ref.shape
    w = s_ref.shape[2]
    w2 = w // 2

    def chunk_lhs(r0):
        return jnp.concatenate([u_ref[t, r0:r0 + rt, :] for t in range(t_len)], axis=-1)

    for r0 in range(0, rows, rt):
        res = _dot(chunk_lhs(r0), kst_ref[0])
        s_ref[r0 // SUBLANE:(r0 + rt) // SUBLANE] = res.reshape(rt // SUBLANE, SUBLANE, w)

    ar = ar_ref[...]
    ai = ai_ref[...]

    def step(hr, hi, sr, si):
        return ar * hr - ai * hi + sr, ar * hi + ai * hr + si

    if nseq % SUBLANE == 0:
        nb = nseq // SUBLANE
        d0 = (lax.broadcasted_iota(jnp.int32, (nb, SUBLANE, w2), 2) & (LANE // 2)) == 0
        hr0 = h0_ref[:, 0:w2].reshape(nb, SUBLANE, w2)
        hi0 = h0_ref[:, w2:w].reshape(nb, SUBLANE, w2)

        def body(k, carry):
            hr, hi = carry
            k1 = nchunk - 1 - k
            b0r = s_ref[pl.ds(k * nb, nb), :, 0:w2]
            b0i = s_ref[pl.ds(k * nb, nb), :, w2:w]
            b1r = s_ref[pl.ds(k1 * nb, nb), :, 0:w2]
            b1i = s_ref[pl.ds(k1 * nb, nb), :, w2:w]
            s_ref[pl.ds(k * nb, nb), :, 0:w2] = jnp.where(d0, hr, b0r)
            s_ref[pl.ds(k * nb, nb), :, w2:w] = jnp.where(d0, hi, b0i)
            s_ref[pl.ds(k1 * nb, nb), :, 0:w2] = jnp.where(d0, b1r, hr)
            s_ref[pl.ds(k1 * nb, nb), :, w2:w] = jnp.where(d0, b1i, hi)
            return step(hr, hi, jnp.where(d0, b0r, b1r), jnp.where(d0, b0i, b1i))

        assert nchunk % 2 == 0
        hr, hi = lax.fori_loop(0, nchunk, body, (hr0, hi0))
        hfin_ref[:, 0:w2] = hr.reshape(nseq, w2)
        hfin_ref[:, w2:w] = hi.reshape(nseq, w2)
    else:
        half = SUBLANE // 2
        assert nseq == half and nchunk % 4 == 0
        nblk = nchunk // 2
        d0 = (lax.broadcasted_iota(jnp.int32, (half, w2), 1) & (LANE // 2)) == 0
        lo = slice(0, half)
        up = slice(half, SUBLANE)

        def body(k, carry):
            hr, hi = carry
            k1 = nblk - 1 - k
            b0r = s_ref[k, :, 0:w2]
            b0i = s_ref[k, :, w2:w]
            b1r = s_ref[k1, :, 0:w2]
            b1i = s_ref[k1, :, w2:w]
            for fa, ba in ((lo, up), (up, lo)):
                s_ref[k, fa, 0:w2] = jnp.where(d0, hr, b0r[fa])
                s_ref[k, fa, w2:w] = jnp.where(d0, hi, b0i[fa])
                s_ref[k1, ba, 0:w2] = jnp.where(d0, b1r[ba], hr)
                s_ref[k1, ba, w2:w] = jnp.where(d0, b1i[ba], hi)
                hr, hi = step(hr, hi, jnp.where(d0, b0r[fa], b1r[ba]), jnp.where(d0, b0i[fa], b1i[ba]))
            return hr, hi

        hr, hi = lax.fori_loop(0, nblk, body, (h0_ref[:, 0:w2], h0_ref[:, w2:w]))
        hfin_ref[:, 0:w2] = hr
        hfin_ref[:, w2:w] = hi

    for r0 in range(0, rows, rt):
        hp = s_ref[r0 // SUBLANE:(r0 + rt) // SUBLANE].reshape(rt, w).astype(BF16)
        y = _dot(chunk_lhs(r0), kfull_ref[0]) + _dot(hp, kout_ref[0])
        for t in range(t_len):
            y_ref[t, r0:r0 + rt, :] = y[:, t * LANE:(t + 1) * LANE].astype(y_ref.dtype)


def _s5(u_t, kst, kfull, kout, ar, ai, h0, *, nseq):
    t_len, rows, _ = u_t.shape
    nblk, kdim, sdim = kst.shape
    nchunk = rows // nseq
    kern = functools.partial(_s5_kernel, nseq=nseq, nchunk=nchunk, rt=_tile(rows, 512))
    return pl.pallas_call(
        kern,
        grid=(nblk,),
        in_specs=[
            pl.BlockSpec((t_len, rows, LANE), lambda i: (0, 0, i)),
            _resident((1, kdim, sdim), lambda i: (i, 0, 0)),
            _resident((1, kdim, kdim), lambda i: (i, 0, 0)),
            _resident((1, sdim, kdim), lambda i: (i, 0, 0)),
            pl.BlockSpec((1, sdim // 2), lambda i: (0, i)),
            pl.BlockSpec((1, sdim // 2), lambda i: (0, i)),
            pl.BlockSpec((nseq, sdim), lambda i: (0, i)),
        ],
        out_specs=[pl.BlockSpec((t_len, rows, LANE), lambda i: (0, 0, i)),
                   pl.BlockSpec((nseq, sdim), lambda i: (0, i))],
        out_shape=[jax.ShapeDtypeStruct(u_t.shape, BF16),
                   jax.ShapeDtypeStruct((nseq, nblk * sdim), F32)],
        scratch_shapes=[pltpu.VMEM((rows // SUBLANE, SUBLANE, sdim), F32)],
        name="s5_seq%d" % nseq,
        compiler_params=_cparams(("arbitrary",)),
    )(u_t, kst, kfull, kout, ar, ai, h0)


def _s5_weights(lam_re, lam_im, log_dt, b_re, b_im, c_re, c_im, d_skip, gb):
    t_len = S5_CHUNK
    _, ngrp, p_len = lam_re.shape
    h_len = b_re.shape[-1]
    dt = jnp.exp(log_dt)[..., None]
    e = jnp.exp(lam_re * dt)
    lbr = e * jnp.cos(lam_im * dt)
    lbi = e * jnp.sin(lam_im * dt)
    den = lam_re * lam_re + lam_im * lam_im
    fr = ((lbr - 1.0) * lam_re + lbi * lam_im) / den
    fi = (lbi * lam_re - (lbr - 1.0) * lam_im) / den
    br = fr[..., None] * b_re - fi[..., None] * b_im
    bi = fr[..., None] * b_im + fi[..., None] * b_re
    pr = [jnp.ones_like(lbr)]
    pi = [jnp.zeros_like(lbr)]
    for _ in range(t_len):
        pr.append(pr[-1] * lbr - pi[-1] * lbi)
        pi.append(pr[-2] * lbi + pi[-1] * lbr)
    pwr = jnp.stack(pr)
    pwi = jnp.stack(pi)

    tt = np.arange(t_len)
    er = jnp.stack([pwr[t_len - 1 - tt, 0], pwr[tt, 1]], axis=1)
    ei = jnp.stack([pwi[t_len - 1 - tt, 0], pwi[tt, 1]], axis=1)
    ks_re = er[..., None] * br[None] - ei[..., None] * bi[None]
    ks_im = er[..., None] * bi[None] + ei[..., None] * br[None]

    nblk = ngrp // gb
    kdim = t_len * gb * h_len
    sdim = gb * 2 * p_len
    grp_of_k = (jnp.arange(kdim) // h_len) % gb
    grp_of_s = jnp.arange(sdim) // (2 * p_len)

    def st_cols(x):
        x = jnp.transpose(x.reshape(t_len, 2, nblk, gb, p_len, h_len), (2, 0, 3, 5, 1, 4))
        x = jnp.tile(x.reshape(nblk, kdim, 2 * p_len), (1, 1, gb))
        return jnp.where(grp_of_k[:, None] == grp_of_s[None, :], x, 0.0)

    kst = jnp.concatenate([st_cols(ks_re), st_cols(ks_im)], axis=-1)

    qr = jnp.stack([pwr[tt + 1, 0], pwr[t_len - tt, 1]], axis=1)
    qi = jnp.stack([pwi[tt + 1, 0], pwi[t_len - tt, 1]], axis=1)
    cqr = c_re[None] * qr[:, :, :, None, :] - c_im[None] * qi[:, :, :, None, :]
    cqi = c_re[None] * qi[:, :, :, None, :] + c_im[None] * qr[:, :, :, None, :]

    def out_rows(x):
        x = jnp.transpose(x.reshape(t_len, 2, nblk, gb, h_len, p_len), (2, 1, 5, 0, 3, 4))
        x = jnp.tile(x.reshape(nblk, 2 * p_len, kdim), (1, gb, 1))
        return jnp.where(grp_of_s[:, None] == grp_of_k[None, :], x, 0.0)

    kout = jnp.concatenate([out_rows(cqr), out_rows(-cqi)], axis=1)

    cpr = c_re[None] * pwr[:t_len, :, :, None, :] - c_im[None] * pwi[:t_len, :, :, None, :]
    cpi = c_re[None] * pwi[:t_len, :, :, None, :] + c_im[None] * pwr[:t_len, :, :, None, :]
    m = jnp.einsum('kdghp,dgpj->kdghj', jnp.concatenate([cpr, -cpi], axis=-1),
                   jnp.concatenate([br, bi], axis=2), precision=HI)
    mz = jnp.concatenate([m, jnp.zeros_like(m[:1])], axis=0)
    lag = tt[None, :] - tt[:, None]
    f_idx = np.where(lag >= 0, lag, t_len)
    b_idx = np.where(lag <= 0, -lag, t_len)
    full = mz[f_idx, 0] + mz[b_idx, 1]
    eye_t = jnp.asarray(np.eye(t_len, dtype=np.float32))
    eye_h = jnp.asarray(np.eye(h_len, dtype=np.float32))
    full = full + (eye_t[:, :, None, None, None] * d_skip[None, None, :, :, None] * eye_h[None, None, None])
    full = jnp.transpose(full.reshape(t_len, t_len, nblk, gb, h_len, h_len), (2, 0, 3, 5, 1, 4))
    full = full.reshape(nblk, kdim, t_len * h_len)
    kk = np.arange(kdim)
    th_of_k = (kk // (gb * h_len)) * h_len + kk % h_len
    spread = jnp.asarray((np.arange(t_len * h_len)[:, None] == th_of_k[None, :]).astype(np.float32))
    full = jnp.einsum('nrk,kc->nrc', full, spread, precision=HI)
    kfull = jnp.where(grp_of_k[:, None] == grp_of_k[None, :], full, 0.0)

    ar = jnp.transpose(pwr[t_len], (1, 0, 2)).reshape(1, ngrp * 2 * p_len)
    ai = jnp.transpose(pwi[t_len], (1, 0, 2)).reshape(1, ngrp * 2 * p_len)
    return kst.astype(BF16), kfull.astype(BF16), kout.astype(BF16), ar, ai


def _state_to_cols(s_re, s_im, gb):
    b, _, ngrp, p_len = s_re.shape
    def blk(x):
        return jnp.transpose(x, (0, 2, 1, 3)).reshape(b, ngrp // gb, 1, gb * 2 * p_len)
    return jnp.concatenate([blk(s_re), blk(s_im)], axis=2).reshape(b, ngrp * 4 * p_len)


def _cols_to_state(cols, ngrp, p_len, gb):
    b = cols.shape[0]
    x = cols.reshape(b, ngrp // gb, 2, gb, 2, p_len)
    x = jnp.transpose(x, (2, 0, 4, 1, 3, 5)).reshape(2, b, 2, ngrp, p_len)
    return x[0], x[1]


def _to_chunks(u, nseq, seq_len):
    nchunk = seq_len // S5_CHUNK
    x = u.reshape(nseq, nchunk, S5_CHUNK, u.shape[-1])
    return jnp.transpose(x, (2, 1, 0, 3)).reshape(S5_CHUNK, nchunk * nseq, u.shape[-1]).astype(BF16)


def _from_chunks(y, nseq, seq_len):
    nchunk = seq_len // S5_CHUNK
    x = y.reshape(S5_CHUNK, nchunk, nseq, y.shape[-1])
    return jnp.transpose(x, (2, 1, 0, 3)).reshape(nseq * seq_len, y.shape[-1])


def _ctx_attn_kernel(q_ref, k_ref, v_ref, o_ref, *, nh, dh):
    scale = dh ** -0.5
    for h in range(nh):
        sl = slice(h * dh, (h + 1) * dh)
        s = _dot_nt(q_ref[:, sl].astype(BF16), k_ref[:, sl].astype(BF16)) * scale
        e = jnp.exp(s - jnp.max(s, axis=-1, keepdims=True))
        o = _dot(e.astype(BF16), v_ref[:, sl].astype(BF16)) / jnp.sum(e, axis=-1, keepdims=True)
        o_ref[:, sl] = o.astype(o_ref.dtype)


def _ctx_attn(z, *, nbatch, seq, nh, dcols):
    kern = functools.partial(_ctx_attn_kernel, nh=nh, dh=dcols // nh)
    return pl.pallas_call(
        kern,
        grid=(nbatch,),
        in_specs=[pl.BlockSpec((seq, dcols), lambda b, c=c: (b, c)) for c in (1, 2, 3)],
        out_specs=pl.BlockSpec((seq, dcols), lambda b: (b, 0)),
        out_shape=jax.ShapeDtypeStruct((nbatch * seq, dcols), BF16),
        name="ctx_attn",
        compiler_params=_cparams(("arbitrary",)),
    )(z, z, z)


def _na_window_start(i, rows, win_r):
    nkr = win_r + NA_QROWS
    first = jnp.clip(NA_QROWS * i - win_r // 2, 0, rows - win_r)
    return jnp.minimum(first, rows - nkr)


def _na_kernel(q_ref, k_ref, v_ref, kc_ref, vc_ref, bias_ref, o_ref, kb_ref, vb_ref, *, rows, win_r, dh, hb):
    i = pl.program_id(2)

    @pl.when(i == 0)
    def _():
        kb_ref[...] = k_ref[...].astype(BF16)
        vb_ref[...] = v_ref[...].astype(BF16)

    nk = (win_r + NA_QROWS) * GRID_W
    start = pl.multiple_of(_na_window_start(i, rows, win_r) * GRID_W, GRID_W)
    outs = []
    for h in range(hb):
        sl = slice(h * dh, (h + 1) * dh)
        q = (q_ref[:, sl] * (dh ** -0.5)).astype(BF16)
        s_loc = _dot_nt(q, kb_ref[pl.ds(start, nk), sl]) + bias_ref[0, h]
        s_ctx = _dot_nt(q, kc_ref[:, sl])
        m = jnp.maximum(jnp.max(s_loc, axis=-1, keepdims=True), jnp.max(s_ctx, axis=-1, keepdims=True))
        e_loc = jnp.exp(s_loc - m)
        e_ctx = jnp.exp(s_ctx - m)
        den = jnp.sum(e_loc, axis=-1, keepdims=True) + jnp.sum(e_ctx, axis=-1, keepdims=True)
        o = _dot(e_loc.astype(BF16), vb_ref[pl.ds(start, nk), sl]) + _dot(e_ctx.astype(BF16), vc_ref[:, sl])
        outs.append((o / den).astype(o_ref.dtype))
    o_ref[...] = jnp.concatenate(outs, axis=-1)


def _na_bias(rpb, rows):
    nh, nir, nic = rpb.shape
    win_r = (nir + 1) // 2
    win_c = (nic + 1) // 2
    wr = min(win_r, rows)
    nkr = wr + NA_QROWS
    nblk = rows // NA_QROWS
    col = np.arange(GRID_W)
    c0 = np.clip(col - win_c // 2, 0, GRID_W - win_c)
    col_ok = (col[None, :] >= c0[:, None]) & (col[None, :] < c0[:, None] + win_c)
    ic = np.clip(col[None, :] - col[:, None] + (win_c - 1), 0, 2 * win_c - 2)

    def row_pattern(i):
        start = min(int(np.clip(NA_QROWS * i - wr // 2, 0, rows - wr)), rows - nkr)
        r = NA_QROWS * i + np.arange(NA_QROWS)
        kr = start + np.arange(nkr)
        kstart = np.clip(r - wr // 2, 0, rows - wr)
        row_ok = (kr[None, :] >= kstart[:, None]) & (kr[None, :] < kstart[:, None] + wr)
        ir = np.clip(kr[None, :] - r[:, None] + (win_r - 1), 0, nir - 1)
        assert row_ok.sum(axis=1).min() == wr
        return row_ok, np.where(row_ok, ir, 0)

    for i in range(2, nblk - 1):
        assert all(np.array_equal(a, b) for a, b in zip(row_pattern(i), row_pattern(1)))

    sel_c = jnp.asarray((ic[None] == np.arange(nic)[:, None, None]).astype(np.float32))
    by_col = jnp.einsum('hic,cqk->hiqk', rpb.astype(F32), sel_c, precision=HI)
    tables = []
    for i in (0, 1, nblk - 1):
        row_ok, ir = row_pattern(i)
        ok = row_ok[:, None, :, None] & col_ok[None, :, None, :]
        sel_r = jnp.asarray((ir[None] == np.arange(nir)[:, None, None]).astype(np.float32))
        vals = jnp.einsum('hiqk,iaj->haqjk', by_col, sel_r, precision=HI)
        vals = jnp.where(jnp.asarray(ok)[None], vals, NEG_INF)
        tables.append(vals.reshape(nh, NA_QROWS * GRID_W, nkr * GRID_W))
    return jnp.stack(tables), wr


def _na_attn(z, kc, vc, bias, *, nbatch, seq, past, nh, row0, win_r, dcols):
    dh = dcols // nh
    rows = seq // GRID_W
    tq = NA_QROWS * GRID_W
    nblk = rows // NA_QROWS
    assert row0 % seq == 0 and rows % NA_QROWS == 0 and rows >= win_r + NA_QROWS
    hb = _tile(nh, NA_HEADS)
    hw = hb * dh
    ng = nh // hb
    kern = functools.partial(_na_kernel, rows=rows, win_r=win_r, dh=dh, hb=hb)
    qoff = row0 // tq
    boff = row0 // seq

    def cls(i):
        return jnp.where(i == 0, 0, jnp.where(i == nblk - 1, 2, 1))

    c_spec = pl.BlockSpec((past, hw), lambda b, h, i: (b, h))
    return pl.pallas_call(
        kern,
        grid=(nbatch, ng, nblk),
        in_specs=[
            pl.BlockSpec((tq, hw), lambda b, h, i: (qoff + b * nblk + i, ng + h)),
            pl.BlockSpec((seq, hw), lambda b, h, i: (boff + b, 2 * ng + h)),
            pl.BlockSpec((seq, hw), lambda b, h, i: (boff + b, 3 * ng + h)),
            c_spec, c_spec,
            pl.BlockSpec((1, hb) + bias.shape[2:], lambda b, h, i: (cls(i), h, 0, 0)),
        ],
        out_specs=pl.BlockSpec((tq, hw), lambda b, h, i: (b * nblk + i, h)),
        out_shape=jax.ShapeDtypeStruct((nbatch * seq, dcols), BF16),
        scratch_shapes=[pltpu.VMEM((seq, hw), BF16), pltpu.VMEM((seq, hw), BF16)],
        name="na_attn",
        compiler_params=_cparams(("arbitrary", "arbitrary", "arbitrary")),
    )(z, z, z, kc, vc, bias)


def _even_out_kernel(ys_ref, yb_ref, x_ref, m_ref, g_ref, wglu_ref, woa_ref, wob_ref, o_ref):
    y = jax.nn.gelu(ys_ref[...])
    gate = jax.nn.sigmoid(_dot(y.astype(BF16), wglu_ref[...]))
    ya = (y * gate).astype(BF16)
    o = _dot(ya, woa_ref[...]) + _dot(yb_ref[...], wob_ref[...])
    r = _rms(o) * g_ref[3:4, :]
    o_ref[...] = x_ref[...] + m_ref[0, 5:6, :] * r


def _even_out(ys, yb, x, mod_l, g_l, w_glu, w_out, *, gid, tm):
    n, d = x.shape
    da = ys.shape[1]
    db = yb.shape[1]
    assert da == db
    return pl.pallas_call(
        _even_out_kernel,
        grid=(n // tm,),
        in_specs=[
            pl.BlockSpec((tm, da), lambda i: (i, 0)),
            pl.BlockSpec((tm, db), lambda i: (i, 0)),
            pl.BlockSpec((tm, d), lambda i: (i, 0)),
            pl.BlockSpec((1, N_MOD, d), lambda i: (gid(i), 0, 0)),
            pl.BlockSpec(g_l.shape, lambda i: (0, 0)),
            _resident((da, da), lambda i: (0, 0)),
            _resident((da, d), lambda i: (0, 0)),
            _resident((db, d), lambda i: (1, 0)),
        ],
        out_specs=pl.BlockSpec((tm, d), lambda i: (i, 0)),
        out_shape=jax.ShapeDtypeStruct((n, d), F32),
        name="even_out",
        compiler_params=_cparams(("arbitrary",)),
    )(ys, yb, x, mod_l, g_l, w_glu, w_out, w_out)


def _odd_in_kernel(x_ref, m_ref, g_ref, wb_ref, wc_ref, wx_ref, b_ref, p_ref, h_ref):
    def step(first):
        h = _mixer_in_h(first, x_ref, m_ref, g_ref, h_ref)
        b_ref[...] = _dot(h, wb_ref[...])
        p_ref[...] = _dot(h, wc_ref[...]) * _dot(h, wx_ref[...])

    j = pl.program_id(1)
    pl.when(j == 0)(functools.partial(step, True))
    pl.when(j > 0)(functools.partial(step, False))


def _odd_in(x, mod_l, g_l, w, *, gid, tm):
    n, d = x.shape
    dc = w.shape[1] // 3
    tn = _tile(dc, 512)
    nj = dc // tn
    blk = pl.BlockSpec((tm, tn), lambda i, j: (i, j))
    return pl.pallas_call(
        _odd_in_kernel,
        grid=(n // tm, nj),
        in_specs=[
            pl.BlockSpec((tm, d), lambda i, j: (i, 0)),
            pl.BlockSpec((1, N_MOD, d), lambda i, j: (gid(i), 0, 0)),
            pl.BlockSpec(g_l.shape, lambda i, j: (0, 0)),
            pl.BlockSpec((d, tn), lambda i, j: (0, j)),
            pl.BlockSpec((d, tn), lambda i, j: (0, j + nj)),
            pl.BlockSpec((d, tn), lambda i, j: (0, j + 2 * nj)),
        ],
        out_specs=[blk, blk],
        out_shape=[jax.ShapeDtypeStruct((n, dc), F32), jax.ShapeDtypeStruct((n, dc), F32)],
        scratch_shapes=[pltpu.VMEM((tm, d), BF16)],
        name="odd_in",
        compiler_params=_cparams(("arbitrary", "arbitrary")),
    )(x, mod_l, g_l, w, w, w)


def _odd_out_kernel(b_ref, p_ref, pp_ref, pn_ref, cw_ref, x_ref, m_ref, g_ref, w_ref, o_ref,
                    *, tm, ctx_rows, ctx_seq, lat_seq):
    i = pl.program_id(0)
    p = p_ref[...]
    row = lax.broadcasted_iota(jnp.int32, (tm, 1), 0)
    grow = i * tm + row
    is_ctx = grow < ctx_rows
    pos = jnp.where(is_ctx, grow % ctx_seq, (grow - ctx_rows) % lat_seq)
    last = jnp.where(is_ctx, ctx_seq - 1, lat_seq - 1)
    prev = pltpu.roll(p, 1, axis=0)
    prev = jnp.where(row == 0, pp_ref[SUBLANE - 1:SUBLANE, :], prev)
    prev = jnp.where(pos == 0, 0.0, prev)
    nxt = pltpu.roll(p, tm - 1, axis=0)
    nxt = jnp.where(row == tm - 1, pn_ref[0:1, :], nxt)
    nxt = jnp.where(pos == last, 0.0, nxt)
    y = cw_ref[0:1, :] * prev + cw_ref[1:2, :] * p + cw_ref[2:3, :] * nxt
    v = (b_ref[...] * y).astype(BF16)
    o = _dot(v, w_ref[...])
    r = _rms(o) * g_ref[3:4, :]
    o_ref[...] = x_ref[...] + m_ref[0, 5:6, :] * r


def _odd_out(b, p, conv_w, x, mod_l, g_l, w, *, gid, tm, ctx_rows, ctx_seq, lat_seq):
    n, d = x.shape
    dc = b.shape[1]
    assert conv_w.shape[0] == 3
    kern = functools.partial(_odd_out_kernel, tm=tm, ctx_rows=ctx_rows, ctx_seq=ctx_seq, lat_seq=lat_seq)
    tpb = tm // SUBLANE
    nb8 = n // SUBLANE
    return pl.pallas_call(
        kern,
        grid=(n // tm,),
        in_specs=[
            pl.BlockSpec((tm, dc), lambda i: (i, 0)),
            pl.BlockSpec((tm, dc), lambda i: (i, 0)),
            pl.BlockSpec((SUBLANE, dc), lambda i: (jnp.maximum(i * tpb - 1, 0), 0)),
            pl.BlockSpec((SUBLANE, dc), lambda i: (jnp.minimum((i + 1) * tpb, nb8 - 1), 0)),
            pl.BlockSpec(conv_w.shape, lambda i: (0, 0)),
            pl.BlockSpec((tm, d), lambda i: (i, 0)),
            pl.BlockSpec((1, N_MOD, d), lambda i: (gid(i), 0, 0)),
            pl.BlockSpec(g_l.shape, lambda i: (0, 0)),
            _resident((dc, d), lambda i: (0, 0)),
        ],
        out_specs=pl.BlockSpec((tm, d), lambda i: (i, 0)),
        out_shape=jax.ShapeDtypeStruct((n, d), F32),
        name="odd_out",
        compiler_params=_cparams(("arbitrary",)),
    )(b, p, p, p, conv_w, x, mod_l, g_l, w)


def kernel(x_prompt, x_sample, cache_k, cache_v, state_s5_re, state_s5_im, c, c_ctx, norm_g, w_mod, b_mod,
           w_ffn_in, w_ffn_out, w_in_even, w_out_even, s5_lam_re, s5_lam_im, s5_log_dt, s5_b_re, s5_b_im,
           s5_c_re, s5_c_im, s5_d, w_glu, na_rpb, w_in_conv, conv_w, w_out_conv):
    nb_c, seq_c, d = x_prompt.shape
    nb_l, seq_l, _ = x_sample.shape
    depth = norm_g.shape[0]
    past, nh = cache_k.shape[2], cache_k.shape[3]
    ngrp, p_len = state_s5_re.shape[3], state_s5_re.shape[4]
    h_len = s5_b_re.shape[-1]
    d_a = ngrp * h_len
    ctx_rows = nb_c * seq_c
    n = ctx_rows + nb_l * seq_l
    tm_w = _tile(math.gcd(ctx_rows, seq_l), 1024)
    tm = _tile(tm_w, 512)
    tm_c = _tile(tm, 256)
    gb = LANE // h_len
    assert gb * h_len == LANE and ngrp % gb == 0 and 2 * p_len == LANE and 2 * d_a == d

    def group_of_tile(rows_per_tile):
        def gid(i):
            r0 = i * rows_per_tile
            return jnp.where(r0 < ctx_rows, 0, 1 + jnp.maximum(r0 - ctx_rows, 0) // seq_l)
        return gid

    gid = group_of_tile(tm)
    gid_c = group_of_tile(tm_c)
    gid_w = group_of_tile(tm_w)

    ng = 1 + nb_l
    ng8 = -(-ng // SUBLANE) * SUBLANE
    cvec = jnp.concatenate([c_ctx[None], c, jnp.zeros((ng8 - ng, d), F32)], axis=0)
    mod = _modulation(cvec, w_mod, b_mod).reshape(depth, ng8, N_MOD, d)

    x = jnp.concatenate([x_prompt.reshape(ctx_rows, d), x_sample.reshape(nb_l * seq_l, d)], axis=0)

    w_ffn_in_b = w_ffn_in.astype(BF16)
    w_ffn_out_b = w_ffn_out.astype(BF16)

    new_k, new_v, new_sre, new_sim = [], [], [], []
    for l in range(depth):
        g_l = norm_g[l]
        mod_l = mod[l]
        x = _ffn(x, mod_l, g_l, w_ffn_in_b, w_ffn_out_b, l=l, k=0, mi=0, gi=0, gid=gid, tm=tm)
        if l % 2 == 0:
            e = l // 2
            z = _even_in(x, mod_l, g_l, w_in_even[e].astype(BF16), gid=gid_w, tm=tm_w)
            new_k.append(z[:ctx_rows, 2 * d_a:3 * d_a].reshape(nb_c, seq_c, nh, -1))
            new_v.append(z[:ctx_rows, 3 * d_a:].reshape(nb_c, seq_c, nh, -1))
            kst, kfull, kout, ar, ai = _s5_weights(s5_lam_re[e], s5_lam_im[e], s5_log_dt[e], s5_b_re[e],
                                                   s5_b_im[e], s5_c_re[e], s5_c_im[e], s5_d[e], gb)
            h0_c = jnp.zeros((nb_c, ngrp * 4 * p_len), F32)
            ys_c, hfin = _s5(_to_chunks(z[:ctx_rows, :d_a], nb_c, seq_c), kst, kfull, kout, ar, ai,
                             h0_c, nseq=nb_c)
            fin_re, fin_im = _cols_to_state(hfin, ngrp, p_len, gb)
            new_sre.append(fin_re)
            new_sim.append(fin_im)
            h0_l = _state_to_cols(state_s5_re[:, e], state_s5_im[:, e], gb)
            ys_l, _ = _s5(_to_chunks(z[ctx_rows:, :d_a], nb_l, seq_l), kst, kfull, kout, ar, ai,
                          h0_l, nseq=nb_l)
            ys = jnp.concatenate([_from_chunks(ys_c, nb_c, seq_c), _from_chunks(ys_l, nb_l, seq_l)], axis=0)
            yb_c = _ctx_attn(z, nbatch=nb_c, seq=seq_c, nh=nh, dcols=d_a)
            bias, win_r = _na_bias(na_rpb[e], seq_l // GRID_W)
            yb_l = _na_attn(z, cache_k[:, e].reshape(nb_l * past, -1).astype(BF16),
                            cache_v[:, e].reshape(nb_l * past, -1).astype(BF16), bias,
                            nbatch=nb_l, seq=seq_l, past=past, nh=nh, row0=ctx_rows, win_r=win_r, dcols=d_a)
            yb = jnp.concatenate([yb_c, yb_l], axis=0)
            x = _even_out(ys, yb, x, mod_l, g_l, w_glu[e].astype(BF16), w_out_even[e].astype(BF16),
                          gid=gid, tm=tm)
        else:
            o = l // 2
            b, p = _odd_in(x, mod_l, g_l, w_in_conv[o].astype(BF16), gid=gid_w, tm=tm_w)
            x = _odd_out(b, p, conv_w[o], x, mod_l, g_l, w_out_conv[o].astype(BF16), gid=gid_c, tm=tm_c,
                         ctx_rows=ctx_rows, ctx_seq=seq_c, lat_seq=seq_l)
        ffn2 = functools.partial(_ffn, x, mod_l, g_l, w_ffn_in_b, w_ffn_out_b, l=l, k=1,
                                 mi=6, gi=4, gid=gid, tm=tm)
        if l < depth - 1:
            x = ffn2()
        else:
            y_prompt = ffn2(row0=0, nrows=ctx_rows).reshape(nb_c, seq_c, d)
            y_sample = ffn2(row0=ctx_rows, nrows=n - ctx_rows).reshape(nb_l, seq_l, d)

    return (y_prompt, y_sample, jnp.stack(new_k, axis=1), jnp.stack(new_v, axis=1),
            jnp.stack(new_sre, axis=1), jnp.stack(new_sim, axis=1))
```

```python
import functools
import math

import numpy as np
import jax
import jax.numpy as jnp
from jax import lax
from jax.experimental import pallas as pl
from jax.experimental.pallas import tpu as pltpu

F32 = jnp.float32
BF16 = jnp.bfloat16

NORM_EPS = 1e-6
N_MOD = 9
GRID_W = 64
NEG_INF = -1e30
S5_CHUNK = 8
NA_QROWS = 4
NA_HEADS = 2
LANE = 128
SUBLANE = 8
VMEM_LIMIT_BYTES = 56 * 1024 * 1024
HI = lax.Precision.HIGHEST


def _cparams(sem):
    return pltpu.CompilerParams(dimension_semantics=sem, vmem_limit_bytes=VMEM_LIMIT_BYTES)


def _tile(n, pref):
    t = min(n, pref)
    while n % t:
        t //= 2
    return t


def _resident(block_shape, index_map):
    return pl.BlockSpec(block_shape, index_map, pipeline_mode=pl.Buffered(1))


def _rms(x):
    return x * lax.rsqrt(jnp.mean(x * x, axis=-1, keepdims=True) + NORM_EPS)


def _modnorm(x, g, shift, scale):
    return (_rms(x) * g) * (1.0 + scale) + shift


def _dot(a, b):
    return jnp.dot(a, b, preferred_element_type=F32)


def _dot_nt(a, b):
    return lax.dot_general(a, b, (((1,), (1,)), ((), ())), preferred_element_type=F32)


def _mod_kernel(c_ref, w_ref, b_ref, o_ref):
    c = c_ref[...]
    s = (c * jax.nn.sigmoid(c)).astype(BF16)
    o_ref[0] = _dot(s, w_ref[0].astype(BF16)) + b_ref[0]


def _modulation(cvec, w_mod, b_mod):
    depth, d, nd = w_mod.shape
    ng = cvec.shape[0]
    tn = _tile(nd, 1024)
    return pl.pallas_call(
        _mod_kernel,
        grid=(depth, nd // tn),
        in_specs=[
            pl.BlockSpec((ng, d), lambda l, j: (0, 0)),
            pl.BlockSpec((1, d, tn), lambda l, j: (l, 0, j)),
            pl.BlockSpec((1, 1, tn), lambda l, j: (l, 0, j)),
        ],
        out_specs=pl.BlockSpec((1, ng, tn), lambda l, j: (l, 0, j)),
        out_shape=jax.ShapeDtypeStruct((depth, ng, nd), F32),
        name="modulation",
        compiler_params=_cparams(("arbitrary", "arbitrary")),
    )(cvec, w_mod, b_mod.reshape(depth, 1, nd))


def _ffn_kernel(x_ref, m_ref, g_ref, wg_ref, wu_ref, wo_ref, *rest, mi, gi, nj):
    o_ref, h_ref = rest[-2:]
    j = pl.program_id(1)

    def step(first, last):
        if first:
            h = _modnorm(x_ref[...], g_ref[gi:gi + 1, :], m_ref[0, mi:mi + 1, :],
                         m_ref[0, mi + 1:mi + 2, :]).astype(BF16)
            h_ref[...] = h
        else:
            h = h_ref[...]
        a = _dot(h, wg_ref[0, 0])
        u = _dot(h, wu_ref[0, 0])
        hid = (a * jax.nn.sigmoid(a) * u).astype(BF16)
        acc = _dot(hid, wo_ref[0, 0])
        if not first:
            acc = o_ref[...] + acc
        if last:
            r = _rms(acc) * g_ref[gi + 1:gi + 2, :]
            acc = x_ref[...] + 0.5 * m_ref[0, mi + 2:mi + 3, :] * r
        o_ref[...] = acc

    if nj == 1:
        step(True, True)
        return
    pl.when(j == 0)(functools.partial(step, True, False))
    if nj > 2:
        pl.when(jnp.logical_and(j > 0, j < nj - 1))(functools.partial(step, False, False))
    pl.when(j == nj - 1)(functools.partial(step, False, True))


def _ffn(x, mod_l, g_l, w_in, w_out, *, l, k, mi, gi, gid, tm, row0=0, nrows=None, stream_row0=None,
         out_rows=None, out_row0=0, into=None):
    d = x.shape[1]
    n = x.shape[0] if nrows is None else nrows
    t0 = row0 // tm
    s0 = t0 if stream_row0 is None else stream_row0 // tm
    o0 = out_row0 // tm
    dff = w_out.shape[2]
    tf = _tile(dff, 512)
    nj = dff // tf
    kern = functools.partial(_ffn_kernel, mi=mi, gi=gi, nj=nj)
    in_specs = [
        pl.BlockSpec((tm, d), lambda i, j: (t0 + i, 0)),
        pl.BlockSpec((1, N_MOD, d), lambda i, j: (gid(s0 + i), 0, 0)),
        pl.BlockSpec(g_l.shape, lambda i, j: (0, 0)),
        pl.BlockSpec((1, 1, d, tf), lambda i, j: (l, k, 0, j)),
        pl.BlockSpec((1, 1, d, tf), lambda i, j: (l, k, 0, j + nj)),
        pl.BlockSpec((1, 1, tf, d), lambda i, j: (l, k, j, 0)),
    ]
    args = [x, mod_l, g_l, w_in, w_in, w_out]
    aliases = {}
    if into is not None:
        aliases = {len(args): 0}
        in_specs.append(pl.BlockSpec(memory_space=pl.ANY))
        args.append(into)
    return pl.pallas_call(
        kern,
        grid=(n // tm, nj),
        in_specs=in_specs,
        out_specs=pl.BlockSpec((tm, d), lambda i, j: (o0 + i, 0)),
        out_shape=jax.ShapeDtypeStruct((n if out_rows is None else out_rows, d), F32),
        scratch_shapes=[pltpu.VMEM((tm, d), BF16)],
        input_output_aliases=aliases,
        name="ffn",
        compiler_params=_cparams(("arbitrary", "arbitrary")),
    )(*args)


def _mixer_in_h(first, x_ref, m_ref, g_ref, h_ref):
    if first:
        h = _modnorm(x_ref[...], g_ref[2:3, :], m_ref[0, 3:4, :], m_ref[0, 4:5, :]).astype(BF16)
        h_ref[...] = h
        return h
    return h_ref[...]


def _even_in_kernel(x_ref, m_ref, g_ref, w_ref, z_ref, u_ref, h_ref):
    def step(first):
        z = _dot(_mixer_in_h(first, x_ref, m_ref, g_ref, h_ref), w_ref[...])
        z_ref[...] = z
        if first:
            u_ref[...] = z.astype(u_ref.dtype)

    j = pl.program_id(1)
    pl.when(j == 0)(functools.partial(step, True))
    pl.when(j > 0)(functools.partial(step, False))


def _even_in(x, mod_l, g_l, w, *, gid, tm, du):
    n, d = x.shape
    nz = w.shape[1]
    tn = du
    return pl.pallas_call(
        _even_in_kernel,
        grid=(n // tm, nz // tn),
        in_specs=[
            pl.BlockSpec((tm, d), lambda i, j: (i, 0)),
            pl.BlockSpec((1, N_MOD, d), lambda i, j: (gid(i), 0, 0)),
            pl.BlockSpec(g_l.shape, lambda i, j: (0, 0)),
            pl.BlockSpec((d, tn), lambda i, j: (0, j)),
        ],
        out_specs=[pl.BlockSpec((tm, tn), lambda i, j: (i, j)),
                   pl.BlockSpec((tm, du), lambda i, j: (i, 0))],
        out_shape=[jax.ShapeDtypeStruct((n, nz), F32), jax.ShapeDtypeStruct((n, du), BF16)],
        scratch_shapes=[pltpu.VMEM((tm, d), BF16)],
        name="even_in",
        compiler_params=_cparams(("arbitrary", "arbitrary")),
    )(x, mod_l, g_l, w)


def _s5_kernel(u_ref, kst_ref, kfull_ref, kout_ref, ar_ref, ai_ref, h0_ref, y_ref, hfin_ref, s_ref,
               *, nseq, nchunk, rt):
    t_len, rows, _ = u_ref.shape
    w = s_ref.shape[2]
    w2 = w // 2

    def chunk_lhs(r0):
        return jnp.concatenate([u_ref[t, r0:r0 + rt, :] for t in range(t_len)], axis=-1)

    for r0 in range(0, rows, rt):
        res = _dot(chunk_lhs(r0), kst_ref[0])
        s_ref[r0 // SUBLANE:(r0 + rt) // SUBLANE] = res.reshape(rt // SUBLANE, SUBLANE, w)

    ar = ar_ref[...]
    ai = ai_ref[...]

    def step(hr, hi, sr, si):
        return ar * hr - ai * hi + sr, ar * hi + ai * hr + si

    if nseq % SUBLANE == 0:
        nb = nseq // SUBLANE
        d0 = (lax.broadcasted_iota(jnp.int32, (nb, SUBLANE, w2), 2) & (LANE // 2)) == 0
        hr0 = h0_ref[:, 0:w2].reshape(nb, SUBLANE, w2)
        hi0 = h0_ref[:, w2:w].reshape(nb, SUBLANE, w2)

        def body(k, carry):
            hr, hi = carry
            k1 = nchunk - 1 - k
            b0r = s_ref[pl.ds(k * nb, nb), :, 0:w2]
            b0i = s_ref[pl.ds(k * nb, nb), :, w2:w]
            b1r = s_ref[pl.ds(k1 * nb, nb), :, 0:w2]
            b1i = s_ref[pl.ds(k1 * nb, nb), :, w2:w]
            s_ref[pl.ds(k * nb, nb), :, 0:w2] = jnp.where(d0, hr, b0r)
            s_ref[pl.ds(k * nb, nb), :, w2:w] = jnp.where(d0, hi, b0i)
            s_ref[pl.ds(k1 * nb, nb), :, 0:w2] = jnp.where(d0, b1r, hr)
            s_ref[pl.ds(k1 * nb, nb), :, w2:w] = jnp.where(d0, b1i, hi)
            return step(hr, hi, jnp.where(d0, b0r, b1r), jnp.where(d0, b0i, b1i))

        assert nchunk % 2 == 0
        hr, hi = lax.fori_loop(0, nchunk, body, (hr0, hi0))
        hfin_ref[:, 0:w2] = hr.reshape(nseq, w2)
        hfin_ref[:, w2:w] = hi.reshape(nseq, w2)
    else:
        half = SUBLANE // 2
        assert nseq == half and nchunk % 4 == 0
        nblk = nchunk // 2
        d0 = (lax.broadcasted_iota(jnp.int32, (half, w2), 1) & (LANE // 2)) == 0
        lo = slice(0, half)
        up = slice(half, SUBLANE)

        def body(k, carry):
            hr, hi = carry
            k1 = nblk - 1 - k
            b0r = s_ref[k, :, 0:w2]
            b0i = s_ref[k, :, w2:w]
            b1r = s_ref[k1, :, 0:w2]
            b1i = s_ref[k1, :, w2:w]
            for fa, ba in ((lo, up), (up, lo)):
                s_ref[k, fa, 0:w2] = jnp.where(d0, hr, b0r[fa])
                s_ref[k, fa, w2:w] = jnp.where(d0, hi, b0i[fa])
                s_ref[k1, ba, 0:w2] = jnp.where(d0, b1r[ba], hr)
                s_ref[k1, ba, w2:w] = jnp.where(d0, b1i[ba], hi)
                hr, hi = step(hr, hi, jnp.where(d0, b0r[fa], b1r[ba]), jnp.where(d0, b0i[fa], b1i[ba]))
            return hr, hi

        hr, hi = lax.fori_loop(0, nblk, body, (h0_ref[:, 0:w2], h0_ref[:, w2:w]))
        hfin_ref[:, 0:w2] = hr
        hfin_ref[:, w2:w] = hi

    for r0 in range(0, rows, rt):
        hp = s_ref[r0 // SUBLANE:(r0 + rt) // SUBLANE].reshape(rt, w).astype(BF16)
        y = _dot(chunk_lhs(r0), kfull_ref[0]) + _dot(hp, kout_ref[0])
        for t in range(t_len):
            y_ref[t, r0:r0 + rt, :] = y[:, t * LANE:(t + 1) * LANE].astype(y_ref.dtype)


def _s5(u_t, kst, kfull, kout, ar, ai, h0, *, nseq):
    t_len, rows, _ = u_t.shape
    nblk, kdim, sdim = kst.shape
    nchunk = rows // nseq
    kern = functools.partial(_s5_kernel, nseq=nseq, nchunk=nchunk, rt=_tile(rows, 512))
    state_bytes = rows * sdim * 4
    wspec = pl.BlockSpec if 4 * state_bytes <= VMEM_LIMIT_BYTES else _resident
    return pl.pallas_call(
        kern,
        grid=(nblk,),
        in_specs=[
            pl.BlockSpec((t_len, rows, LANE), lambda i: (0, 0, i)),
            wspec((1, kdim, sdim), lambda i: (i, 0, 0)),
            wspec((1, kdim, kdim), lambda i: (i, 0, 0)),
            wspec((1, sdim, kdim), lambda i: (i, 0, 0)),
            pl.BlockSpec((1, sdim // 2), lambda i: (0, i)),
            pl.BlockSpec((1, sdim // 2), lambda i: (0, i)),
            pl.BlockSpec((nseq, sdim), lambda i: (0, i)),
        ],
        out_specs=[pl.BlockSpec((t_len, rows, LANE), lambda i: (0, 0, i)),
                   pl.BlockSpec((nseq, sdim), lambda i: (0, i))],
        out_shape=[jax.ShapeDtypeStruct(u_t.shape, BF16),
                   jax.ShapeDtypeStruct((nseq, nblk * sdim), F32)],
        scratch_shapes=[pltpu.VMEM((rows // SUBLANE, SUBLANE, sdim), F32)],
        name="s5_seq%d" % nseq,
        compiler_params=_cparams(("arbitrary",)),
    )(u_t, kst, kfull, kout, ar, ai, h0)


def _s5_weights(lam_re, lam_im, log_dt, b_re, b_im, c_re, c_im, d_skip, gb):
    t_len = S5_CHUNK
    _, ngrp, p_len = lam_re.shape
    h_len = b_re.shape[-1]
    dt = jnp.exp(log_dt)[..., None]
    e = jnp.exp(lam_re * dt)
    lbr = e * jnp.cos(lam_im * dt)
    lbi = e * jnp.sin(lam_im * dt)
    den = lam_re * lam_re + lam_im * lam_im
    fr = ((lbr - 1.0) * lam_re + lbi * lam_im) / den
    fi = (lbi * lam_re - (lbr - 1.0) * lam_im) / den
    br = fr[..., None] * b_re - fi[..., None] * b_im
    bi = fr[..., None] * b_im + fi[..., None] * b_re
    pr = [jnp.ones_like(lbr)]
    pi = [jnp.zeros_like(lbr)]
    for _ in range(t_len):
        pr.append(pr[-1] * lbr - pi[-1] * lbi)
        pi.append(pr[-2] * lbi + pi[-1] * lbr)
    pwr = jnp.stack(pr)
    pwi = jnp.stack(pi)

    tt = np.arange(t_len)
    er = jnp.stack([pwr[t_len - 1 - tt, 0], pwr[tt, 1]], axis=1)
    ei = jnp.stack([pwi[t_len - 1 - tt, 0], pwi[tt, 1]], axis=1)
    ks_re = er[..., None] * br[None] - ei[..., None] * bi[None]
    ks_im = er[..., None] * bi[None] + ei[..., None] * br[None]

    nblk = ngrp // gb
    kdim = t_len * gb * h_len
    sdim = gb * 2 * p_len
    grp_of_k = (jnp.arange(kdim) // h_len) % gb
    grp_of_s = jnp.arange(sdim) // (2 * p_len)

    def st_cols(x):
        x = jnp.transpose(x.reshape(t_len, 2, nblk, gb, p_len, h_len), (2, 0, 3, 5, 1, 4))
        x = jnp.tile(x.reshape(nblk, kdim, 2 * p_len), (1, 1, gb))
        return jnp.where(grp_of_k[:, None] == grp_of_s[None, :], x, 0.0)

    kst = jnp.concatenate([st_cols(ks_re), st_cols(ks_im)], axis=-1)

    qr = jnp.stack([pwr[tt + 1, 0], pwr[t_len - tt, 1]], axis=1)
    qi = jnp.stack([pwi[tt + 1, 0], pwi[t_len - tt, 1]], axis=1)
    cqr = c_re[None] * qr[:, :, :, None, :] - c_im[None] * qi[:, :, :, None, :]
    cqi = c_re[None] * qi[:, :, :, None, :] + c_im[None] * qr[:, :, :, None, :]

    def out_rows(x):
        x = jnp.transpose(x.reshape(t_len, 2, nblk, gb, h_len, p_len), (2, 1, 5, 0, 3, 4))
        x = jnp.tile(x.reshape(nblk, 2 * p_len, kdim), (1, gb, 1))
        return jnp.where(grp_of_s[:, None] == grp_of_k[None, :], x, 0.0)

    kout = jnp.concatenate([out_rows(cqr), out_rows(-cqi)], axis=1)

    cpr = c_re[None] * pwr[:t_len, :, :, None, :] - c_im[None] * pwi[:t_len, :, :, None, :]
    cpi = c_re[None] * pwi[:t_len, :, :, None, :] + c_im[None] * pwr[:t_len, :, :, None, :]
    m = jnp.einsum('kdghp,dgpj->kdghj', jnp.concatenate([cpr, -cpi], axis=-1),
                   jnp.concatenate([br, bi], axis=2), precision=HI)
    mz = jnp.concatenate([m, jnp.zeros_like(m[:1])], axis=0)
    lag = tt[None, :] - tt[:, None]
    f_idx = np.where(lag >= 0, lag, t_len)
    b_idx = np.where(lag <= 0, -lag, t_len)
    full = mz[f_idx, 0] + mz[b_idx, 1]
    eye_t = jnp.asarray(np.eye(t_len, dtype=np.float32))
    eye_h = jnp.asarray(np.eye(h_len, dtype=np.float32))
    full = full + (eye_t[:, :, None, None, None] * d_skip[None, None, :, :, None] * eye_h[None, None, None])
    full = jnp.transpose(full.reshape(t_len, t_len, nblk, gb, h_len, h_len), (2, 0, 3, 5, 1, 4))
    full = full.reshape(nblk, kdim, t_len * h_len)
    kk = np.arange(kdim)
    th_of_k = (kk // (gb * h_len)) * h_len + kk % h_len
    spread = jnp.asarray((np.arange(t_len * h_len)[:, None] == th_of_k[None, :]).astype(np.float32))
    full = jnp.einsum('nrk,kc->nrc', full, spread, precision=HI)
    kfull = jnp.where(grp_of_k[:, None] == grp_of_k[None, :], full, 0.0)

    ar = jnp.transpose(pwr[t_len], (1, 0, 2)).reshape(1, ngrp * 2 * p_len)
    ai = jnp.transpose(pwi[t_len], (1, 0, 2)).reshape(1, ngrp * 2 * p_len)
    return kst.astype(BF16), kfull.astype(BF16), kout.astype(BF16), ar, ai


def _state_to_cols(s_re, s_im, gb):
    b, _, ngrp, p_len = s_re.shape
    def blk(x):
        return jnp.transpose(x, (0, 2, 1, 3)).reshape(b, ngrp // gb, 1, gb * 2 * p_len)
    return jnp.concatenate([blk(s_re), blk(s_im)], axis=2).reshape(b, ngrp * 4 * p_len)


def _cols_to_state(cols, ngrp, p_len, gb):
    b = cols.shape[0]
    x = cols.reshape(b, ngrp // gb, 2, gb, 2, p_len)
    x = jnp.transpose(x, (2, 0, 4, 1, 3, 5)).reshape(2, b, 2, ngrp, p_len)
    return x[0], x[1]


def _to_chunks(u, nseq, seq_len):
    nchunk = seq_len // S5_CHUNK
    x = u.reshape(nseq, nchunk, S5_CHUNK, u.shape[-1])
    return jnp.transpose(x, (2, 1, 0, 3)).reshape(S5_CHUNK, nchunk * nseq, u.shape[-1]).astype(BF16)


def _from_chunks(y, nseq, seq_len):
    nchunk = seq_len // S5_CHUNK
    x = y.reshape(S5_CHUNK, nchunk, nseq, y.shape[-1])
    return jnp.transpose(x, (2, 1, 0, 3)).reshape(nseq * seq_len, y.shape[-1])


def _ctx_attn_kernel(q_ref, k_ref, v_ref, o_ref, *, nh, dh):
    scale = dh ** -0.5
    for h in range(nh):
        sl = slice(h * dh, (h + 1) * dh)
        s = _dot_nt(q_ref[:, sl].astype(BF16), k_ref[:, sl].astype(BF16)) * scale
        e = jnp.exp(s - jnp.max(s, axis=-1, keepdims=True))
        o = _dot(e.astype(BF16), v_ref[:, sl].astype(BF16)) / jnp.sum(e, axis=-1, keepdims=True)
        o_ref[:, sl] = o.astype(o_ref.dtype)


def _ctx_attn(z, *, nbatch, seq, nh, dcols):
    kern = functools.partial(_ctx_attn_kernel, nh=nh, dh=dcols // nh)
    return pl.pallas_call(
        kern,
        grid=(nbatch,),
        in_specs=[pl.BlockSpec((seq, dcols), lambda b, c=c: (b, c)) for c in (1, 2, 3)],
        out_specs=pl.BlockSpec((seq, dcols), lambda b: (b, 0)),
        out_shape=jax.ShapeDtypeStruct((nbatch * seq, dcols), BF16),
        name="ctx_attn",
        compiler_params=_cparams(("arbitrary",)),
    )(z, z, z)


def _na_window_start(i, rows, win_r):
    nkr = win_r + NA_QROWS
    first = jnp.clip(NA_QROWS * i - win_r // 2, 0, rows - win_r)
    return jnp.minimum(first, rows - nkr)


def _na_kernel(q_ref, k_ref, v_ref, kc_ref, vc_ref, bias_ref, o_ref, kb_ref, vb_ref, *, rows, win_r, dh, hb):
    i = pl.program_id(2)

    @pl.when(i == 0)
    def _():
        kb_ref[...] = k_ref[...].astype(BF16)
        vb_ref[...] = v_ref[...].astype(BF16)

    nk = (win_r + NA_QROWS) * GRID_W
    start = pl.multiple_of(_na_window_start(i, rows, win_r) * GRID_W, GRID_W)
    outs = []
    for h in range(hb):
        sl = slice(h * dh, (h + 1) * dh)
        q = (q_ref[:, sl] * (dh ** -0.5)).astype(BF16)
        s_loc = _dot_nt(q, kb_ref[pl.ds(start, nk), sl]) + bias_ref[0, h]
        s_ctx = _dot_nt(q, kc_ref[:, sl])
        m = jnp.maximum(jnp.max(s_loc, axis=-1, keepdims=True), jnp.max(s_ctx, axis=-1, keepdims=True))
        e_loc = jnp.exp(s_loc - m)
        e_ctx = jnp.exp(s_ctx - m)
        den = jnp.sum(e_loc, axis=-1, keepdims=True) + jnp.sum(e_ctx, axis=-1, keepdims=True)
        o = _dot(e_loc.astype(BF16), vb_ref[pl.ds(start, nk), sl]) + _dot(e_ctx.astype(BF16), vc_ref[:, sl])
        outs.append((o / den).astype(o_ref.dtype))
    o_ref[...] = jnp.concatenate(outs, axis=-1)


def _na_bias(rpb, rows):
    nh, nir, nic = rpb.shape
    win_r = (nir + 1) // 2
    win_c = (nic + 1) // 2
    wr = min(win_r, rows)
    nkr = wr + NA_QROWS
    nblk = rows // NA_QROWS
    col = np.arange(GRID_W)
    c0 = np.clip(col - win_c // 2, 0, GRID_W - win_c)
    col_ok = (col[None, :] >= c0[:, None]) & (col[None, :] < c0[:, None] + win_c)
    ic = np.clip(col[None, :] - col[:, None] + (win_c - 1), 0, 2 * win_c - 2)

    def row_pattern(i):
        start = min(int(np.clip(NA_QROWS * i - wr // 2, 0, rows - wr)), rows - nkr)
        r = NA_QROWS * i + np.arange(NA_QROWS)
        kr = start + np.arange(nkr)
        kstart = np.clip(r - wr // 2, 0, rows - wr)
        row_ok = (kr[None, :] >= kstart[:, None]) & (kr[None, :] < kstart[:, None] + wr)
        ir = np.clip(kr[None, :] - r[:, None] + (win_r - 1), 0, nir - 1)
        assert row_ok.sum(axis=1).min() == wr
        return row_ok, np.where(row_ok, ir, 0)

    for i in range(2, nblk - 1):
        assert all(np.array_equal(a, b) for a, b in zip(row_pattern(i), row_pattern(1)))

    sel_c = jnp.asarray((ic[None] == np.arange(nic)[:, None, None]).astype(np.float32))
    by_col = jnp.einsum('hic,cqk->hiqk', rpb.astype(F32), sel_c, precision=HI)
    tables = []
    for i in (0, 1, nblk - 1):
        row_ok, ir = row_pattern(i)
        ok = row_ok[:, None, :, None] & col_ok[None, :, None, :]
        sel_r = jnp.asarray((ir[None] == np.arange(nir)[:, None, None]).astype(np.float32))
        vals = jnp.einsum('hiqk,iaj->haqjk', by_col, sel_r, precision=HI)
        vals = jnp.where(jnp.asarray(ok)[None], vals, NEG_INF)
        tables.append(vals.reshape(nh, NA_QROWS * GRID_W, nkr * GRID_W))
    return jnp.stack(tables), wr


def _na_attn(z, kc, vc, bias, *, nbatch, seq, past, nh, row0, win_r, dcols):
    dh = dcols // nh
    rows = seq // GRID_W
    tq = NA_QROWS * GRID_W
    nblk = rows // NA_QROWS
    assert row0 % seq == 0 and rows % NA_QROWS == 0 and rows >= win_r + NA_QROWS
    hb = _tile(nh, NA_HEADS)
    hw = hb * dh
    ng = nh // hb
    kern = functools.partial(_na_kernel, rows=rows, win_r=win_r, dh=dh, hb=hb)
    qoff = row0 // tq
    boff = row0 // seq

    def cls(i):
        return jnp.where(i == 0, 0, jnp.where(i == nblk - 1, 2, 1))

    c_spec = pl.BlockSpec((past, hw), lambda b, h, i: (b, h))
    return pl.pallas_call(
        kern,
        grid=(nbatch, ng, nblk),
        in_specs=[
            pl.BlockSpec((tq, hw), lambda b, h, i: (qoff + b * nblk + i, ng + h)),
            pl.BlockSpec((seq, hw), lambda b, h, i: (boff + b, 2 * ng + h)),
            pl.BlockSpec((seq, hw), lambda b, h, i: (boff + b, 3 * ng + h)),
            c_spec, c_spec,
            pl.BlockSpec((1, hb) + bias.shape[2:], lambda b, h, i: (cls(i), h, 0, 0)),
        ],
        out_specs=pl.BlockSpec((tq, hw), lambda b, h, i: (b * nblk + i, h)),
        out_shape=jax.ShapeDtypeStruct((nbatch * seq, dcols), BF16),
        scratch_shapes=[pltpu.VMEM((seq, hw), BF16), pltpu.VMEM((seq, hw), BF16)],
        name="na_attn",
        compiler_params=_cparams(("arbitrary", "arbitrary", "arbitrary")),
    )(z, z, z, kc, vc, bias)


def _even_out_kernel(ysc_ref, ysl_ref, ybc_ref, ybl_ref, x_ref, m_ref, g_ref, wglu_ref, woa_ref, wob_ref,
                     o_ref, *, nc):
    is_ctx = pl.program_id(0) < nc
    ys = jnp.where(is_ctx, ysc_ref[...], ysl_ref[...])
    yb = jnp.where(is_ctx, ybc_ref[...], ybl_ref[...])
    y = jax.nn.gelu(ys.astype(F32))
    gate = jax.nn.sigmoid(_dot(y.astype(BF16), wglu_ref[...]))
    ya = (y * gate).astype(BF16)
    o = _dot(ya, woa_ref[...]) + _dot(yb, wob_ref[...])
    r = _rms(o) * g_ref[3:4, :]
    o_ref[...] = x_ref[...] + m_ref[0, 5:6, :] * r


def _even_out(ys_c, ys_l, yb_c, yb_l, x, mod_l, g_l, w_glu, w_out, *, gid, tm):
    n, d = x.shape
    da = ys_c.shape[1]
    db = yb_c.shape[1]
    nc = ys_c.shape[0] // tm
    assert da == db and nc * tm == ys_c.shape[0] == yb_c.shape[0]

    def ctx_blk(i):
        return (jnp.minimum(i, nc - 1), 0)

    def lat_blk(i):
        return (jnp.maximum(i - nc, 0), 0)

    return pl.pallas_call(
        functools.partial(_even_out_kernel, nc=nc),
        grid=(n // tm,),
        in_specs=[
            pl.BlockSpec((tm, da), ctx_blk),
            pl.BlockSpec((tm, da), lat_blk),
            pl.BlockSpec((tm, db), ctx_blk),
            pl.BlockSpec((tm, db), lat_blk),
            pl.BlockSpec((tm, d), lambda i: (i, 0)),
            pl.BlockSpec((1, N_MOD, d), lambda i: (gid(i), 0, 0)),
            pl.BlockSpec(g_l.shape, lambda i: (0, 0)),
            _resident((da, da), lambda i: (0, 0)),
            _resident((da, d), lambda i: (0, 0)),
            _resident((db, d), lambda i: (1, 0)),
        ],
        out_specs=pl.BlockSpec((tm, d), lambda i: (i, 0)),
        out_shape=jax.ShapeDtypeStruct((n, d), F32),
        name="even_out",
        compiler_params=_cparams(("arbitrary",)),
    )(ys_c, ys_l, yb_c, yb_l, x, mod_l, g_l, w_glu, w_out, w_out)


def _odd_in_kernel(x_ref, m_ref, g_ref, wb_ref, wc_ref, wx_ref, b_ref, p_ref, h_ref):
    def step(first):
        h = _mixer_in_h(first, x_ref, m_ref, g_ref, h_ref)
        b_ref[...] = _dot(h, wb_ref[...])
        p_ref[...] = _dot(h, wc_ref[...]) * _dot(h, wx_ref[...])

    j = pl.program_id(1)
    pl.when(j == 0)(functools.partial(step, True))
    pl.when(j > 0)(functools.partial(step, False))


def _odd_in(x, mod_l, g_l, w, *, gid, tm):
    n, d = x.shape
    dc = w.shape[1] // 3
    tn = _tile(dc, 512)
    nj = dc // tn
    blk = pl.BlockSpec((tm, tn), lambda i, j: (i, j))
    return pl.pallas_call(
        _odd_in_kernel,
        grid=(n // tm, nj),
        in_specs=[
            pl.BlockSpec((tm, d), lambda i, j: (i, 0)),
            pl.BlockSpec((1, N_MOD, d), lambda i, j: (gid(i), 0, 0)),
            pl.BlockSpec(g_l.shape, lambda i, j: (0, 0)),
            pl.BlockSpec((d, tn), lambda i, j: (0, j)),
            pl.BlockSpec((d, tn), lambda i, j: (0, j + nj)),
            pl.BlockSpec((d, tn), lambda i, j: (0, j + 2 * nj)),
        ],
        out_specs=[blk, blk],
        out_shape=[jax.ShapeDtypeStruct((n, dc), F32), jax.ShapeDtypeStruct((n, dc), F32)],
        scratch_shapes=[pltpu.VMEM((tm, d), BF16)],
        name="odd_in",
        compiler_params=_cparams(("arbitrary", "arbitrary")),
    )(x, mod_l, g_l, w, w, w)


def _odd_out_kernel(b_ref, p_ref, pp_ref, pn_ref, cw_ref, x_ref, m_ref, g_ref, w_ref, o_ref,
                    *, tm, ctx_rows, ctx_seq, lat_seq):
    i = pl.program_id(0)
    p = p_ref[...]
    row = lax.broadcasted_iota(jnp.int32, (tm, 1), 0)
    grow = i * tm + row
    is_ctx = grow < ctx_rows
    pos = jnp.where(is_ctx, grow % ctx_seq, (grow - ctx_rows) % lat_seq)
    last = jnp.where(is_ctx, ctx_seq - 1, lat_seq - 1)
    prev = pltpu.roll(p, 1, axis=0)
    prev = jnp.where(row == 0, pp_ref[SUBLANE - 1:SUBLANE, :], prev)
    prev = jnp.where(pos == 0, 0.0, prev)
    nxt = pltpu.roll(p, tm - 1, axis=0)
    nxt = jnp.where(row == tm - 1, pn_ref[0:1, :], nxt)
    nxt = jnp.where(pos == last, 0.0, nxt)
    y = cw_ref[0:1, :] * prev + cw_ref[1:2, :] * p + cw_ref[2:3, :] * nxt
    v = (b_ref[...] * y).astype(BF16)
    o = _dot(v, w_ref[...])
    r = _rms(o) * g_ref[3:4, :]
    o_ref[...] = x_ref[...] + m_ref[0, 5:6, :] * r


def _odd_out(b, p, conv_w, x, mod_l, g_l, w, *, gid, tm, ctx_rows, ctx_seq, lat_seq):
    n, d = x.shape
    dc = b.shape[1]
    assert conv_w.shape[0] == 3
    kern = functools.partial(_odd_out_kernel, tm=tm, ctx_rows=ctx_rows, ctx_seq=ctx_seq, lat_seq=lat_seq)
    tpb = tm // SUBLANE
    nb8 = n // SUBLANE
    return pl.pallas_call(
        kern,
        grid=(n // tm,),
        in_specs=[
            pl.BlockSpec((tm, dc), lambda i: (i, 0)),
            pl.BlockSpec((tm, dc), lambda i: (i, 0)),
            pl.BlockSpec((SUBLANE, dc), lambda i: (jnp.maximum(i * tpb - 1, 0), 0)),
            pl.BlockSpec((SUBLANE, dc), lambda i: (jnp.minimum((i + 1) * tpb, nb8 - 1), 0)),
            pl.BlockSpec(conv_w.shape, lambda i: (0, 0)),
            pl.BlockSpec((tm, d), lambda i: (i, 0)),
            pl.BlockSpec((1, N_MOD, d), lambda i: (gid(i), 0, 0)),
            pl.BlockSpec(g_l.shape, lambda i: (0, 0)),
            _resident((dc, d), lambda i: (0, 0)),
        ],
        out_specs=pl.BlockSpec((tm, d), lambda i: (i, 0)),
        out_shape=jax.ShapeDtypeStruct((n, d), F32),
        name="odd_out",
        compiler_params=_cparams(("arbitrary",)),
    )(b, p, p, p, conv_w, x, mod_l, g_l, w)


def kernel(x_prompt, x_sample, cache_k, cache_v, state_s5_re, state_s5_im, c, c_ctx, norm_g, w_mod, b_mod,
           w_ffn_in, w_ffn_out, w_in_even, w_out_even, s5_lam_re, s5_lam_im, s5_log_dt, s5_b_re, s5_b_im,
           s5_c_re, s5_c_im, s5_d, w_glu, na_rpb, w_in_conv, conv_w, w_out_conv):
    nb_c, seq_c, d = x_prompt.shape
    nb_l, seq_l, _ = x_sample.shape
    depth = norm_g.shape[0]
    past, nh = cache_k.shape[2], cache_k.shape[3]
    ngrp, p_len = state_s5_re.shape[3], state_s5_re.shape[4]
    h_len = s5_b_re.shape[-1]
    d_a = ngrp * h_len
    ctx_rows = nb_c * seq_c
    n = ctx_rows + nb_l * seq_l
    tm_w = _tile(math.gcd(ctx_rows, seq_l), 1024)
    tm = _tile(tm_w, 512)
    tm_c = _tile(tm, 256)
    gb = LANE // h_len
    assert gb * h_len == LANE and ngrp % gb == 0 and 2 * p_len == LANE and 2 * d_a == d

    def group_of_tile(rows_per_tile):
        def gid(i):
            r0 = i * rows_per_tile
            return jnp.where(r0 < ctx_rows, 0, 1 + jnp.maximum(r0 - ctx_rows, 0) // seq_l)
        return gid

    gid = group_of_tile(tm)
    gid_c = group_of_tile(tm_c)
    gid_w = group_of_tile(tm_w)

    ng = 1 + nb_l
    ng8 = -(-ng // SUBLANE) * SUBLANE
    cvec = jnp.concatenate([c_ctx[None], c, jnp.zeros((ng8 - ng, d), F32)], axis=0)
    mod = _modulation(cvec, w_mod, b_mod).reshape(depth, ng8, N_MOD, d)


    w_ffn_in_b = w_ffn_in.astype(BF16)
    w_ffn_out_b = w_ffn_out.astype(BF16)

    new_k, new_v, new_sre, new_sim = [], [], [], []
    for l in range(depth):
        g_l = norm_g[l]
        mod_l = mod[l]
        ffn1 = functools.partial(_ffn, mod_l=mod_l, g_l=g_l, w_in=w_ffn_in_b, w_out=w_ffn_out_b, l=l, k=0,
                                 mi=0, gi=0, gid=gid, tm=tm)
        if l > 0:
            x = ffn1(x)
        else:
            x = ffn1(x_prompt.reshape(ctx_rows, d), out_rows=n)
            x = ffn1(x_sample.reshape(n - ctx_rows, d), stream_row0=ctx_rows, out_rows=n, out_row0=ctx_rows,
                     into=x)
        if l % 2 == 0:
            e = l // 2
            z, u = _even_in(x, mod_l, g_l, w_in_even[e].astype(BF16), gid=gid_w, tm=tm_w, du=d_a)
            new_k.append(z[:ctx_rows, 2 * d_a:3 * d_a].reshape(nb_c, seq_c, nh, -1))
            new_v.append(z[:ctx_rows, 3 * d_a:].reshape(nb_c, seq_c, nh, -1))
            kst, kfull, kout, ar, ai = _s5_weights(s5_lam_re[e], s5_lam_im[e], s5_log_dt[e], s5_b_re[e],
                                                   s5_b_im[e], s5_c_re[e], s5_c_im[e], s5_d[e], gb)
            h0_c = jnp.zeros((nb_c, ngrp * 4 * p_len), F32)
            ys_c, hfin = _s5(_to_chunks(u[:ctx_rows], nb_c, seq_c), kst, kfull, kout, ar, ai,
                             h0_c, nseq=nb_c)
            fin_re, fin_im = _cols_to_state(hfin, ngrp, p_len, gb)
            new_sre.append(fin_re)
            new_sim.append(fin_im)
            h0_l = _state_to_cols(state_s5_re[:, e], state_s5_im[:, e], gb)
            ys_l, _ = _s5(_to_chunks(u[ctx_rows:], nb_l, seq_l), kst, kfull, kout, ar, ai,
                          h0_l, nseq=nb_l)
            yb_c = _ctx_attn(z, nbatch=nb_c, seq=seq_c, nh=nh, dcols=d_a)
            bias, win_r = _na_bias(na_rpb[e], seq_l // GRID_W)
            yb_l = _na_attn(z, cache_k[:, e].reshape(nb_l * past, -1).astype(BF16),
                            cache_v[:, e].reshape(nb_l * past, -1).astype(BF16), bias,
                            nbatch=nb_l, seq=seq_l, past=past, nh=nh, row0=ctx_rows, win_r=win_r, dcols=d_a)
            x = _even_out(_from_chunks(ys_c, nb_c, seq_c), _from_chunks(ys_l, nb_l, seq_l), yb_c, yb_l,
                          x, mod_l, g_l, w_glu[e].astype(BF16), w_out_even[e].astype(BF16), gid=gid, tm=tm)
        else:
            o = l // 2
            b, p = _odd_in(x, mod_l, g_l, w_in_conv[o].astype(BF16), gid=gid_w, tm=tm_w)
            x = _odd_out(b, p, conv_w[o], x, mod_l, g_l, w_out_conv[o].astype(BF16), gid=gid_c, tm=tm_c,
                         ctx_rows=ctx_rows, ctx_seq=seq_c, lat_seq=seq_l)
        ffn2 = functools.partial(_ffn, x, mod_l, g_l, w_ffn_in_b, w_ffn_out_b, l=l, k=1,
                                 mi=6, gi=4, gid=gid, tm=tm)
        if l < depth - 1:
            x = ffn2()
        else:
            y_prompt = ffn2(row0=0, nrows=ctx_rows).reshape(nb_c, seq_c, d)
            y_sample = ffn2(row0=ctx_rows, nrows=n - ctx_rows).reshape(nb_l, seq_l, d)

    return (y_prompt, y_sample, jnp.stack(new_k, axis=1), jnp.stack(new_v, axis=1),
            jnp.stack(new_sre, axis=1), jnp.stack(new_sim, axis=1))
```

```python
import functools
import math

import numpy as np
import jax
import jax.numpy as jnp
from jax import lax
from jax.experimental import pallas as pl
from jax.experimental.pallas import tpu as pltpu

F32 = jnp.float32
BF16 = jnp.bfloat16

NORM_EPS = 1e-6
N_MOD = 9
GRID_W = 64
NEG_INF = -1e30
S5_CHUNK = 8
NA_QROWS = 4
NA_HEADS = 4
LANE = 128
SUBLANE = 8
VMEM_LIMIT_BYTES = 56 * 1024 * 1024
HI = lax.Precision.HIGHEST


def _cparams(sem):
    return pltpu.CompilerParams(dimension_semantics=sem, vmem_limit_bytes=VMEM_LIMIT_BYTES)


def _tile(n, pref):
    t = min(n, pref)
    while n % t:
        t //= 2
    return t


def _resident(block_shape, index_map):
    return pl.BlockSpec(block_shape, index_map, pipeline_mode=pl.Buffered(1))


def _rms(x):
    return x * lax.rsqrt(jnp.mean(x * x, axis=-1, keepdims=True) + NORM_EPS)


def _modnorm(x, g, shift, scale):
    return _rms(x) * (g * (1.0 + scale)) + shift


def _post_residual(x, o, g, gate):
    return x + _rms(o) * (gate * g)


def _dot(a, b):
    return jnp.dot(a, b, preferred_element_type=F32)


def _dot_nt(a, b):
    return lax.dot_general(a, b, (((1,), (1,)), ((), ())), preferred_element_type=F32)


def _mod_kernel(c_ref, w_ref, b_ref, o_ref):
    c = c_ref[...]
    s = (c * jax.nn.sigmoid(c)).astype(BF16)
    o_ref[0] = _dot(s, w_ref[0].astype(BF16)) + b_ref[0]


def _modulation(cvec, w_mod, b_mod):
    depth, d, nd = w_mod.shape
    ng = cvec.shape[0]
    tn = _tile(nd, 1024)
    return pl.pallas_call(
        _mod_kernel,
        grid=(depth, nd // tn),
        in_specs=[
            pl.BlockSpec((ng, d), lambda l, j: (0, 0)),
            pl.BlockSpec((1, d, tn), lambda l, j: (l, 0, j)),
            pl.BlockSpec((1, 1, tn), lambda l, j: (l, 0, j)),
        ],
        out_specs=pl.BlockSpec((1, ng, tn), lambda l, j: (l, 0, j)),
        out_shape=jax.ShapeDtypeStruct((depth, ng, nd), F32),
        name="modulation",
        compiler_params=_cparams(("arbitrary", "arbitrary")),
    )(cvec, w_mod, b_mod.reshape(depth, 1, nd))


def _ffn_kernel(x_ref, m_ref, g_ref, wg_ref, wu_ref, wo_ref, *rest, mi, gi, nj):
    o_ref, h_ref = rest[-2:]
    j = pl.program_id(1)

    def step(first, last):
        if first:
            h = _modnorm(x_ref[...], g_ref[gi:gi + 1, :], m_ref[0, mi:mi + 1, :],
                         m_ref[0, mi + 1:mi + 2, :]).astype(BF16)
            h_ref[...] = h
        else:
            h = h_ref[...]
        a = _dot(h, wg_ref[0, 0])
        u = _dot(h, wu_ref[0, 0])
        hid = (a * jax.nn.sigmoid(a) * u).astype(BF16)
        acc = _dot(hid, wo_ref[0, 0])
        if not first:
            acc = o_ref[...] + acc
        if last:
            acc = _post_residual(x_ref[...], acc, g_ref[gi + 1:gi + 2, :], 0.5 * m_ref[0, mi + 2:mi + 3, :])
        o_ref[...] = acc

    if nj == 1:
        step(True, True)
        return
    pl.when(j == 0)(functools.partial(step, True, False))
    if nj > 2:
        pl.when(jnp.logical_and(j > 0, j < nj - 1))(functools.partial(step, False, False))
    pl.when(j == nj - 1)(functools.partial(step, False, True))


def _ffn(x, mod_l, g_l, w_in, w_out, *, l, k, mi, gi, gid, tm, row0=0, nrows=None, stream_row0=None,
         out_rows=None, out_row0=0, into=None):
    d = x.shape[1]
    n = x.shape[0] if nrows is None else nrows
    t0 = row0 // tm
    s0 = t0 if stream_row0 is None else stream_row0 // tm
    o0 = out_row0 // tm
    dff = w_out.shape[2]
    tf = _tile(dff, 512)
    nj = dff // tf
    kern = functools.partial(_ffn_kernel, mi=mi, gi=gi, nj=nj)
    in_specs = [
        pl.BlockSpec((tm, d), lambda i, j: (t0 + i, 0)),
        pl.BlockSpec((1, N_MOD, d), lambda i, j: (gid(s0 + i), 0, 0)),
        pl.BlockSpec(g_l.shape, lambda i, j: (0, 0)),
        pl.BlockSpec((1, 1, d, tf), lambda i, j: (l, k, 0, j)),
        pl.BlockSpec((1, 1, d, tf), lambda i, j: (l, k, 0, j + nj)),
        pl.BlockSpec((1, 1, tf, d), lambda i, j: (l, k, j, 0)),
    ]
    args = [x, mod_l, g_l, w_in, w_in, w_out]
    aliases = {}
    if into is not None:
        aliases = {len(args): 0}
        in_specs.append(pl.BlockSpec(memory_space=pl.ANY))
        args.append(into)
    return pl.pallas_call(
        kern,
        grid=(n // tm, nj),
        in_specs=in_specs,
        out_specs=pl.BlockSpec((tm, d), lambda i, j: (o0 + i, 0)),
        out_shape=jax.ShapeDtypeStruct((n if out_rows is None else out_rows, d), F32),
        scratch_shapes=[pltpu.VMEM((tm, d), BF16)],
        input_output_aliases=aliases,
        name="ffn",
        compiler_params=_cparams(("arbitrary", "arbitrary")),
    )(*args)


def _mixer_in_h(first, x_ref, m_ref, g_ref, h_ref):
    if first:
        h = _modnorm(x_ref[...], g_ref[2:3, :], m_ref[0, 3:4, :], m_ref[0, 4:5, :]).astype(BF16)
        h_ref[...] = h
        return h
    return h_ref[...]


def _even_in_kernel(x_ref, m_ref, g_ref, w_ref, z_ref, u_ref, kv_ref, h_ref):
    def step(first, is_kv):
        z = _dot(_mixer_in_h(first, x_ref, m_ref, g_ref, h_ref), w_ref[...])
        z_ref[...] = z
        if first:
            u_ref[...] = z.astype(u_ref.dtype)
        if is_kv:
            kv_ref[...] = z.astype(kv_ref.dtype)

    j = pl.program_id(1)
    pl.when(j == 0)(functools.partial(step, True, False))
    pl.when(j == 1)(functools.partial(step, False, False))
    pl.when(j >= 2)(functools.partial(step, False, True))


def _even_in(x, mod_l, g_l, w, *, gid, tm, du):
    n, d = x.shape
    nz = w.shape[1]
    tn = du
    assert nz == 4 * du
    return pl.pallas_call(
        _even_in_kernel,
        grid=(n // tm, nz // tn),
        in_specs=[
            pl.BlockSpec((tm, d), lambda i, j: (i, 0)),
            pl.BlockSpec((1, N_MOD, d), lambda i, j: (gid(i), 0, 0)),
            pl.BlockSpec(g_l.shape, lambda i, j: (0, 0)),
            pl.BlockSpec((d, tn), lambda i, j: (0, j)),
        ],
        out_specs=[pl.BlockSpec((tm, tn), lambda i, j: (i, j)),
                   pl.BlockSpec((tm, du), lambda i, j: (i, 0)),
                   pl.BlockSpec((tm, du), lambda i, j: (i, jnp.maximum(j - 2, 0)))],
        out_shape=[jax.ShapeDtypeStruct((n, nz), F32), jax.ShapeDtypeStruct((n, du), BF16),
                   jax.ShapeDtypeStruct((n, 2 * du), BF16)],
        scratch_shapes=[pltpu.VMEM((tm, d), BF16)],
        name="even_in",
        compiler_params=_cparams(("arbitrary", "arbitrary")),
    )(x, mod_l, g_l, w)


def _s5_kernel(u_ref, kst_ref, kfull_ref, kout_ref, ar_ref, ai_ref, h0_ref, y_ref, hfin_ref, s_ref,
               *, nseq, nchunk, rt):
    t_len, rows, _ = u_ref.shape
    w = s_ref.shape[2]
    w2 = w // 2

    def chunk_lhs(r0):
        return jnp.concatenate([u_ref[t, r0:r0 + rt, :] for t in range(t_len)], axis=-1)

    for r0 in range(0, rows, rt):
        res = _dot(chunk_lhs(r0), kst_ref[0])
        s_ref[r0 // SUBLANE:(r0 + rt) // SUBLANE] = res.reshape(rt // SUBLANE, SUBLANE, w)

    ar = ar_ref[...]
    ai = ai_ref[...]

    def step(hr, hi, sr, si):
        return ar * hr - ai * hi + sr, ar * hi + ai * hr + si

    if nseq % SUBLANE == 0:
        nb = nseq // SUBLANE
        d0 = (lax.broadcasted_iota(jnp.int32, (nb, SUBLANE, w2), 2) & (LANE // 2)) == 0
        hr0 = h0_ref[:, 0:w2].reshape(nb, SUBLANE, w2)
        hi0 = h0_ref[:, w2:w].reshape(nb, SUBLANE, w2)

        def body(k, carry):
            hr, hi = carry
            k1 = nchunk - 1 - k
            b0r = s_ref[pl.ds(k * nb, nb), :, 0:w2]
            b0i = s_ref[pl.ds(k * nb, nb), :, w2:w]
            b1r = s_ref[pl.ds(k1 * nb, nb), :, 0:w2]
            b1i = s_ref[pl.ds(k1 * nb, nb), :, w2:w]
            s_ref[pl.ds(k * nb, nb), :, 0:w2] = jnp.where(d0, hr, b0r)
            s_ref[pl.ds(k * nb, nb), :, w2:w] = jnp.where(d0, hi, b0i)
            s_ref[pl.ds(k1 * nb, nb), :, 0:w2] = jnp.where(d0, b1r, hr)
            s_ref[pl.ds(k1 * nb, nb), :, w2:w] = jnp.where(d0, b1i, hi)
            return step(hr, hi, jnp.where(d0, b0r, b1r), jnp.where(d0, b0i, b1i))

        assert nchunk % 2 == 0
        hr, hi = lax.fori_loop(0, nchunk, body, (hr0, hi0))
        hfin_ref[:, 0:w2] = hr.reshape(nseq, w2)
        hfin_ref[:, w2:w] = hi.reshape(nseq, w2)
    else:
        half = SUBLANE // 2
        assert nseq == half and nchunk % 4 == 0
        nblk = nchunk // 2
        d0 = (lax.broadcasted_iota(jnp.int32, (half, w2), 1) & (LANE // 2)) == 0
        lo = slice(0, half)
        up = slice(half, SUBLANE)

        def body(k, carry):
            hr, hi = carry
            k1 = nblk - 1 - k
            b0r = s_ref[k, :, 0:w2]
            b0i = s_ref[k, :, w2:w]
            b1r = s_ref[k1, :, 0:w2]
            b1i = s_ref[k1, :, w2:w]
            for fa, ba in ((lo, up), (up, lo)):
                s_ref[k, fa, 0:w2] = jnp.where(d0, hr, b0r[fa])
                s_ref[k, fa, w2:w] = jnp.where(d0, hi, b0i[fa])
                s_ref[k1, ba, 0:w2] = jnp.where(d0, b1r[ba], hr)
                s_ref[k1, ba, w2:w] = jnp.where(d0, b1i[ba], hi)
                hr, hi = step(hr, hi, jnp.where(d0, b0r[fa], b1r[ba]), jnp.where(d0, b0i[fa], b1i[ba]))
            return hr, hi

        hr, hi = lax.fori_loop(0, nblk, body, (h0_ref[:, 0:w2], h0_ref[:, w2:w]))
        hfin_ref[:, 0:w2] = hr
        hfin_ref[:, w2:w] = hi

    for r0 in range(0, rows, rt):
        hp = s_ref[r0 // SUBLANE:(r0 + rt) // SUBLANE].reshape(rt, w).astype(BF16)
        y = _dot(chunk_lhs(r0), kfull_ref[0]) + _dot(hp, kout_ref[0])
        for t in range(t_len):
            y_ref[t, r0:r0 + rt, :] = y[:, t * LANE:(t + 1) * LANE].astype(y_ref.dtype)


def _s5(u_t, kst, kfull, kout, ar, ai, h0, *, nseq):
    t_len, rows, _ = u_t.shape
    nblk, kdim, sdim = kst.shape
    nchunk = rows // nseq
    kern = functools.partial(_s5_kernel, nseq=nseq, nchunk=nchunk, rt=_tile(rows, 512))
    state_bytes = rows * sdim * 4
    wspec = pl.BlockSpec if 4 * state_bytes <= VMEM_LIMIT_BYTES else _resident
    return pl.pallas_call(
        kern,
        grid=(nblk,),
        in_specs=[
            pl.BlockSpec((t_len, rows, LANE), lambda i: (0, 0, i)),
            wspec((1, kdim, sdim), lambda i: (i, 0, 0)),
            wspec((1, kdim, kdim), lambda i: (i, 0, 0)),
            wspec((1, sdim, kdim), lambda i: (i, 0, 0)),
            pl.BlockSpec((1, sdim // 2), lambda i: (0, i)),
            pl.BlockSpec((1, sdim // 2), lambda i: (0, i)),
            pl.BlockSpec((nseq, sdim), lambda i: (0, i)),
        ],
        out_specs=[pl.BlockSpec((t_len, rows, LANE), lambda i: (0, 0, i)),
                   pl.BlockSpec((nseq, sdim), lambda i: (0, i))],
        out_shape=[jax.ShapeDtypeStruct(u_t.shape, BF16),
                   jax.ShapeDtypeStruct((nseq, nblk * sdim), F32)],
        scratch_shapes=[pltpu.VMEM((rows // SUBLANE, SUBLANE, sdim), F32)],
        name="s5_seq%d" % nseq,
        compiler_params=_cparams(("arbitrary",)),
    )(u_t, kst, kfull, kout, ar, ai, h0)


def _s5_weights(lam_re, lam_im, log_dt, b_re, b_im, c_re, c_im, d_skip, gb):
    t_len = S5_CHUNK
    _, ngrp, p_len = lam_re.shape
    h_len = b_re.shape[-1]
    dt = jnp.exp(log_dt)[..., None]
    e = jnp.exp(lam_re * dt)
    lbr = e * jnp.cos(lam_im * dt)
    lbi = e * jnp.sin(lam_im * dt)
    den = lam_re * lam_re + lam_im * lam_im
    fr = ((lbr - 1.0) * lam_re + lbi * lam_im) / den
    fi = (lbi * lam_re - (lbr - 1.0) * lam_im) / den
    br = fr[..., None] * b_re - fi[..., None] * b_im
    bi = fr[..., None] * b_im + fi[..., None] * b_re
    pr = [jnp.ones_like(lbr)]
    pi = [jnp.zeros_like(lbr)]
    for _ in range(t_len):
        pr.append(pr[-1] * lbr - pi[-1] * lbi)
        pi.append(pr[-2] * lbi + pi[-1] * lbr)
    pwr = jnp.stack(pr)
    pwi = jnp.stack(pi)

    tt = np.arange(t_len)
    er = jnp.stack([pwr[t_len - 1 - tt, 0], pwr[tt, 1]], axis=1)
    ei = jnp.stack([pwi[t_len - 1 - tt, 0], pwi[tt, 1]], axis=1)
    ks_re = er[..., None] * br[None] - ei[..., None] * bi[None]
    ks_im = er[..., None] * bi[None] + ei[..., None] * br[None]

    nblk = ngrp // gb
    kdim = t_len * gb * h_len
    sdim = gb * 2 * p_len
    grp_of_k = (jnp.arange(kdim) // h_len) % gb
    grp_of_s = jnp.arange(sdim) // (2 * p_len)

    def st_cols(x):
        x = jnp.transpose(x.reshape(t_len, 2, nblk, gb, p_len, h_len), (2, 0, 3, 5, 1, 4))
        x = jnp.tile(x.reshape(nblk, kdim, 2 * p_len), (1, 1, gb))
        return jnp.where(grp_of_k[:, None] == grp_of_s[None, :], x, 0.0)

    kst = jnp.concatenate([st_cols(ks_re), st_cols(ks_im)], axis=-1)

    qr = jnp.stack([pwr[tt + 1, 0], pwr[t_len - tt, 1]], axis=1)
    qi = jnp.stack([pwi[tt + 1, 0], pwi[t_len - tt, 1]], axis=1)
    cqr = c_re[None] * qr[:, :, :, None, :] - c_im[None] * qi[:, :, :, None, :]
    cqi = c_re[None] * qi[:, :, :, None, :] + c_im[None] * qr[:, :, :, None, :]

    def out_rows(x):
        x = jnp.transpose(x.reshape(t_len, 2, nblk, gb, h_len, p_len), (2, 1, 5, 0, 3, 4))
        x = jnp.tile(x.reshape(nblk, 2 * p_len, kdim), (1, gb, 1))
        return jnp.where(grp_of_s[:, None] == grp_of_k[None, :], x, 0.0)

    kout = jnp.concatenate([out_rows(cqr), out_rows(-cqi)], axis=1)

    cpr = c_re[None] * pwr[:t_len, :, :, None, :] - c_im[None] * pwi[:t_len, :, :, None, :]
    cpi = c_re[None] * pwi[:t_len, :, :, None, :] + c_im[None] * pwr[:t_len, :, :, None, :]
    m = jnp.einsum('kdghp,dgpj->kdghj', jnp.concatenate([cpr, -cpi], axis=-1),
                   jnp.concatenate([br, bi], axis=2), precision=HI)
    mz = jnp.concatenate([m, jnp.zeros_like(m[:1])], axis=0)
    lag = tt[None, :] - tt[:, None]
    f_idx = np.where(lag >= 0, lag, t_len)
    b_idx = np.where(lag <= 0, -lag, t_len)
    full = mz[f_idx, 0] + mz[b_idx, 1]
    eye_t = jnp.asarray(np.eye(t_len, dtype=np.float32))
    eye_h = jnp.asarray(np.eye(h_len, dtype=np.float32))
    full = full + (eye_t[:, :, None, None, None] * d_skip[None, None, :, :, None] * eye_h[None, None, None])
    full = jnp.transpose(full.reshape(t_len, t_len, nblk, gb, h_len, h_len), (2, 0, 3, 5, 1, 4))
    full = full.reshape(nblk, kdim, t_len * h_len)
    kk = np.arange(kdim)
    th_of_k = (kk // (gb * h_len)) * h_len + kk % h_len
    spread = jnp.asarray((np.arange(t_len * h_len)[:, None] == th_of_k[None, :]).astype(np.float32))
    full = jnp.einsum('nrk,kc->nrc', full, spread, precision=HI)
    kfull = jnp.where(grp_of_k[:, None] == grp_of_k[None, :], full, 0.0)

    ar = jnp.transpose(pwr[t_len], (1, 0, 2)).reshape(1, ngrp * 2 * p_len)
    ai = jnp.transpose(pwi[t_len], (1, 0, 2)).reshape(1, ngrp * 2 * p_len)
    return kst.astype(BF16), kfull.astype(BF16), kout.astype(BF16), ar, ai


def _state_to_cols(s_re, s_im, gb):
    b, _, ngrp, p_len = s_re.shape
    def blk(x):
        return jnp.transpose(x, (0, 2, 1, 3)).reshape(b, ngrp // gb, 1, gb * 2 * p_len)
    return jnp.concatenate([blk(s_re), blk(s_im)], axis=2).reshape(b, ngrp * 4 * p_len)


def _cols_to_state(cols, ngrp, p_len, gb):
    b = cols.shape[0]
    x = cols.reshape(b, ngrp // gb, 2, gb, 2, p_len)
    x = jnp.transpose(x, (2, 0, 4, 1, 3, 5)).reshape(2, b, 2, ngrp, p_len)
    return x[0], x[1]


def _to_chunks(u, nseq, seq_len):
    nchunk = seq_len // S5_CHUNK
    x = u.reshape(nseq, nchunk, S5_CHUNK, u.shape[-1])
    return jnp.transpose(x, (2, 1, 0, 3)).reshape(S5_CHUNK, nchunk * nseq, u.shape[-1]).astype(BF16)


def _from_chunks(y, nseq, seq_len):
    nchunk = seq_len // S5_CHUNK
    x = y.reshape(S5_CHUNK, nchunk, nseq, y.shape[-1])
    return jnp.transpose(x, (2, 1, 0, 3)).reshape(nseq * seq_len, y.shape[-1])


def _ctx_attn_kernel(q_ref, k_ref, v_ref, o_ref, *, nh, dh):
    scale = dh ** -0.5
    for h in range(nh):
        sl = slice(h * dh, (h + 1) * dh)
        s = _dot_nt(q_ref[:, sl].astype(BF16), k_ref[:, sl].astype(BF16)) * scale
        e = jnp.exp(s - jnp.max(s, axis=-1, keepdims=True))
        o = _dot(e.astype(BF16), v_ref[:, sl].astype(BF16)) / jnp.sum(e, axis=-1, keepdims=True)
        o_ref[:, sl] = o.astype(o_ref.dtype)


def _ctx_attn(z, *, nbatch, seq, nh, dcols):
    kern = functools.partial(_ctx_attn_kernel, nh=nh, dh=dcols // nh)
    return pl.pallas_call(
        kern,
        grid=(nbatch,),
        in_specs=[pl.BlockSpec((seq, dcols), lambda b, c=c: (b, c)) for c in (1, 2, 3)],
        out_specs=pl.BlockSpec((seq, dcols), lambda b: (b, 0)),
        out_shape=jax.ShapeDtypeStruct((nbatch * seq, dcols), BF16),
        name="ctx_attn",
        compiler_params=_cparams(("arbitrary",)),
    )(z, z, z)


def _na_window_start(i, rows, win_r):
    nkr = win_r + NA_QROWS
    first = jnp.clip(NA_QROWS * i - win_r // 2, 0, rows - win_r)
    return jnp.minimum(first, rows - nkr)


def _na_kernel(q_ref, kb_ref, vb_ref, kc_ref, vc_ref, bias_ref, o_ref, *, rows, win_r, dh, hb):
    i = pl.program_id(2)
    nk = (win_r + NA_QROWS) * GRID_W
    start = pl.multiple_of(_na_window_start(i, rows, win_r) * GRID_W, GRID_W)
    outs = []
    for h in range(hb):
        sl = slice(h * dh, (h + 1) * dh)
        q = (q_ref[:, sl] * (dh ** -0.5)).astype(BF16)
        s_loc = _dot_nt(q, kb_ref[pl.ds(start, nk), sl]) + bias_ref[0, h]
        s_ctx = _dot_nt(q, kc_ref[:, sl])
        m = jnp.maximum(jnp.max(s_loc, axis=-1, keepdims=True), jnp.max(s_ctx, axis=-1, keepdims=True))
        e_loc = jnp.exp(s_loc - m)
        e_ctx = jnp.exp(s_ctx - m)
        den = jnp.sum(e_loc, axis=-1, keepdims=True) + jnp.sum(e_ctx, axis=-1, keepdims=True)
        o = _dot(e_loc.astype(BF16), vb_ref[pl.ds(start, nk), sl]) + _dot(e_ctx.astype(BF16), vc_ref[:, sl])
        outs.append((o / den).astype(o_ref.dtype))
    o_ref[...] = jnp.concatenate(outs, axis=-1)


def _na_bias(rpb, rows):
    nh, nir, nic = rpb.shape
    win_r = (nir + 1) // 2
    win_c = (nic + 1) // 2
    wr = min(win_r, rows)
    nkr = wr + NA_QROWS
    nblk = rows // NA_QROWS
    col = np.arange(GRID_W)
    c0 = np.clip(col - win_c // 2, 0, GRID_W - win_c)
    col_ok = (col[None, :] >= c0[:, None]) & (col[None, :] < c0[:, None] + win_c)
    ic = np.clip(col[None, :] - col[:, None] + (win_c - 1), 0, 2 * win_c - 2)

    def row_pattern(i):
        start = min(int(np.clip(NA_QROWS * i - wr // 2, 0, rows - wr)), rows - nkr)
        r = NA_QROWS * i + np.arange(NA_QROWS)
        kr = start + np.arange(nkr)
        kstart = np.clip(r - wr // 2, 0, rows - wr)
        row_ok = (kr[None, :] >= kstart[:, None]) & (kr[None, :] < kstart[:, None] + wr)
        ir = np.clip(kr[None, :] - r[:, None] + (win_r - 1), 0, nir - 1)
        assert row_ok.sum(axis=1).min() == wr
        return row_ok, np.where(row_ok, ir, 0)

    for i in range(2, nblk - 1):
        assert all(np.array_equal(a, b) for a, b in zip(row_pattern(i), row_pattern(1)))

    sel_c = jnp.asarray((ic[None] == np.arange(nic)[:, None, None]).astype(np.float32))
    by_col = jnp.einsum('hic,cqk->hiqk', rpb.astype(F32), sel_c, precision=HI)
    tables = []
    for i in (0, 1, nblk - 1):
        row_ok, ir = row_pattern(i)
        ok = row_ok[:, None, :, None] & col_ok[None, :, None, :]
        sel_r = jnp.asarray((ir[None] == np.arange(nir)[:, None, None]).astype(np.float32))
        vals = jnp.einsum('hiqk,iaj->haqjk', by_col, sel_r, precision=HI)
        vals = jnp.where(jnp.asarray(ok)[None], vals, NEG_INF)
        tables.append(vals.reshape(nh, NA_QROWS * GRID_W, nkr * GRID_W))
    return jnp.stack(tables), wr


def _na_attn(z, kv, kc, vc, bias, *, nbatch, seq, past, nh, row0, win_r, dcols):
    dh = dcols // nh
    rows = seq // GRID_W
    tq = NA_QROWS * GRID_W
    nblk = rows // NA_QROWS
    assert row0 % seq == 0 and rows % NA_QROWS == 0 and rows >= win_r + NA_QROWS
    hb = _tile(nh, NA_HEADS)
    hw = hb * dh
    ng = nh // hb
    kern = functools.partial(_na_kernel, rows=rows, win_r=win_r, dh=dh, hb=hb)
    qoff = row0 // tq
    boff = row0 // seq

    def cls(i):
        return jnp.where(i == 0, 0, jnp.where(i == nblk - 1, 2, 1))

    c_spec = pl.BlockSpec((past, hw), lambda b, h, i: (b, h))
    return pl.pallas_call(
        kern,
        grid=(nbatch, ng, nblk),
        in_specs=[
            pl.BlockSpec((tq, hw), lambda b, h, i: (qoff + b * nblk + i, ng + h)),
            pl.BlockSpec((seq, hw), lambda b, h, i: (boff + b, h)),
            pl.BlockSpec((seq, hw), lambda b, h, i: (boff + b, ng + h)),
            c_spec, c_spec,
            pl.BlockSpec((1, hb) + bias.shape[2:], lambda b, h, i: (cls(i), h, 0, 0)),
        ],
        out_specs=pl.BlockSpec((tq, hw), lambda b, h, i: (b * nblk + i, h)),
        out_shape=jax.ShapeDtypeStruct((nbatch * seq, dcols), BF16),
        name="na_attn",
        compiler_params=_cparams(("arbitrary", "arbitrary", "arbitrary")),
    )(z, kv, kv, kc, vc, bias)


def _even_out_kernel(ysc_ref, ysl_ref, ybc_ref, ybl_ref, x_ref, m_ref, g_ref, wglu_ref, woa_ref, wob_ref,
                     o_ref, *, nc):
    is_ctx = pl.program_id(0) < nc
    ys = jnp.where(is_ctx, ysc_ref[...], ysl_ref[...])
    yb = jnp.where(is_ctx, ybc_ref[...], ybl_ref[...])
    y = jax.nn.gelu(ys.astype(F32))
    gate = jax.nn.sigmoid(_dot(y.astype(BF16), wglu_ref[...]))
    ya = (y * gate).astype(BF16)
    o = _dot(ya, woa_ref[...]) + _dot(yb, wob_ref[...])
    o_ref[...] = _post_residual(x_ref[...], o, g_ref[3:4, :], m_ref[0, 5:6, :])


def _even_out(ys_c, ys_l, yb_c, yb_l, x, mod_l, g_l, w_glu, w_out, *, gid, tm):
    n, d = x.shape
    da = ys_c.shape[1]
    db = yb_c.shape[1]
    nc = ys_c.shape[0] // tm
    assert da == db and nc * tm == ys_c.shape[0] == yb_c.shape[0]

    def ctx_blk(i):
        return (jnp.minimum(i, nc - 1), 0)

    def lat_blk(i):
        return (jnp.maximum(i - nc, 0), 0)

    return pl.pallas_call(
        functools.partial(_even_out_kernel, nc=nc),
        grid=(n // tm,),
        in_specs=[
            pl.BlockSpec((tm, da), ctx_blk),
            pl.BlockSpec((tm, da), lat_blk),
            pl.BlockSpec((tm, db), ctx_blk),
            pl.BlockSpec((tm, db), lat_blk),
            pl.BlockSpec((tm, d), lambda i: (i, 0)),
            pl.BlockSpec((1, N_MOD, d), lambda i: (gid(i), 0, 0)),
            pl.BlockSpec(g_l.shape, lambda i: (0, 0)),
            _resident((da, da), lambda i: (0, 0)),
            _resident((da, d), lambda i: (0, 0)),
            _resident((db, d), lambda i: (1, 0)),
        ],
        out_specs=pl.BlockSpec((tm, d), lambda i: (i, 0)),
        out_shape=jax.ShapeDtypeStruct((n, d), F32),
        name="even_out",
        compiler_params=_cparams(("arbitrary",)),
    )(ys_c, ys_l, yb_c, yb_l, x, mod_l, g_l, w_glu, w_out, w_out)


def _odd_in_kernel(x_ref, m_ref, g_ref, wb_ref, wc_ref, wx_ref, b_ref, p_ref, h_ref):
    def step(first):
        h = _mixer_in_h(first, x_ref, m_ref, g_ref, h_ref)
        b_ref[...] = _dot(h, wb_ref[...])
        p_ref[...] = _dot(h, wc_ref[...]) * _dot(h, wx_ref[...])

    j = pl.program_id(1)
    pl.when(j == 0)(functools.partial(step, True))
    pl.when(j > 0)(functools.partial(step, False))


def _odd_in(x, mod_l, g_l, w, *, gid, tm):
    n, d = x.shape
    dc = w.shape[1] // 3
    tn = _tile(dc, 512)
    nj = dc // tn
    blk = pl.BlockSpec((tm, tn), lambda i, j: (i, j))
    return pl.pallas_call(
        _odd_in_kernel,
        grid=(n // tm, nj),
        in_specs=[
            pl.BlockSpec((tm, d), lambda i, j: (i, 0)),
            pl.BlockSpec((1, N_MOD, d), lambda i, j: (gid(i), 0, 0)),
            pl.BlockSpec(g_l.shape, lambda i, j: (0, 0)),
            pl.BlockSpec((d, tn), lambda i, j: (0, j)),
            pl.BlockSpec((d, tn), lambda i, j: (0, j + nj)),
            pl.BlockSpec((d, tn), lambda i, j: (0, j + 2 * nj)),
        ],
        out_specs=[blk, blk],
        out_shape=[jax.ShapeDtypeStruct((n, dc), F32), jax.ShapeDtypeStruct((n, dc), F32)],
        scratch_shapes=[pltpu.VMEM((tm, d), BF16)],
        name="odd_in",
        compiler_params=_cparams(("arbitrary", "arbitrary")),
    )(x, mod_l, g_l, w, w, w)


def _odd_out_kernel(b_ref, p_ref, pp_ref, pn_ref, cw_ref, x_ref, m_ref, g_ref, w_ref, o_ref,
                    *, tm, ctx_rows, ctx_seq, lat_seq):
    i = pl.program_id(0)
    p = p_ref[...]
    row = lax.broadcasted_iota(jnp.int32, (tm, 1), 0)
    grow = i * tm + row
    is_ctx = grow < ctx_rows
    pos = jnp.where(is_ctx, grow % ctx_seq, (grow - ctx_rows) % lat_seq)
    last = jnp.where(is_ctx, ctx_seq - 1, lat_seq - 1)
    prev = pltpu.roll(p, 1, axis=0)
    prev = jnp.where(row == 0, pp_ref[SUBLANE - 1:SUBLANE, :], prev)
    prev = jnp.where(pos == 0, 0.0, prev)
    nxt = pltpu.roll(p, tm - 1, axis=0)
    nxt = jnp.where(row == tm - 1, pn_ref[0:1, :], nxt)
    nxt = jnp.where(pos == last, 0.0, nxt)
    y = cw_ref[0:1, :] * prev + cw_ref[1:2, :] * p + cw_ref[2:3, :] * nxt
    v = (b_ref[...] * y).astype(BF16)
    o = _dot(v, w_ref[...])
    o_ref[...] = _post_residual(x_ref[...], o, g_ref[3:4, :], m_ref[0, 5:6, :])


def _odd_out(b, p, conv_w, x, mod_l, g_l, w, *, gid, tm, ctx_rows, ctx_seq, lat_seq):
    n, d = x.shape
    dc = b.shape[1]
    assert conv_w.shape[0] == 3
    kern = functools.partial(_odd_out_kernel, tm=tm, ctx_rows=ctx_rows, ctx_seq=ctx_seq, lat_seq=lat_seq)
    tpb = tm // SUBLANE
    nb8 = n // SUBLANE
    return pl.pallas_call(
        kern,
        grid=(n // tm,),
        in_specs=[
            pl.BlockSpec((tm, dc), lambda i: (i, 0)),
            pl.BlockSpec((tm, dc), lambda i: (i, 0)),
            pl.BlockSpec((SUBLANE, dc), lambda i: (jnp.maximum(i * tpb - 1, 0), 0)),
            pl.BlockSpec((SUBLANE, dc), lambda i: (jnp.minimum((i + 1) * tpb, nb8 - 1), 0)),
            pl.BlockSpec(conv_w.shape, lambda i: (0, 0)),
            pl.BlockSpec((tm, d), lambda i: (i, 0)),
            pl.BlockSpec((1, N_MOD, d), lambda i: (gid(i), 0, 0)),
            pl.BlockSpec(g_l.shape, lambda i: (0, 0)),
            _resident((dc, d), lambda i: (0, 0)),
        ],
        out_specs=pl.BlockSpec((tm, d), lambda i: (i, 0)),
        out_shape=jax.ShapeDtypeStruct((n, d), F32),
        name="odd_out",
        compiler_params=_cparams(("arbitrary",)),
    )(b, p, p, p, conv_w, x, mod_l, g_l, w)


def kernel(x_prompt, x_sample, cache_k, cache_v, state_s5_re, state_s5_im, c, c_ctx, norm_g, w_mod, b_mod,
           w_ffn_in, w_ffn_out, w_in_even, w_out_even, s5_lam_re, s5_lam_im, s5_log_dt, s5_b_re, s5_b_im,
           s5_c_re, s5_c_im, s5_d, w_glu, na_rpb, w_in_conv, conv_w, w_out_conv):
    nb_c, seq_c, d = x_prompt.shape
    nb_l, seq_l, _ = x_sample.shape
    depth = norm_g.shape[0]
    past, nh = cache_k.shape[2], cache_k.shape[3]
    ngrp, p_len = state_s5_re.shape[3], state_s5_re.shape[4]
    h_len = s5_b_re.shape[-1]
    d_a = ngrp * h_len
    ctx_rows = nb_c * seq_c
    n = ctx_rows + nb_l * seq_l
    tm_w = _tile(math.gcd(ctx_rows, seq_l), 1024)
    tm = _tile(tm_w, 512)
    tm_c = _tile(tm, 256)
    gb = LANE // h_len
    assert gb * h_len == LANE and ngrp % gb == 0 and 2 * p_len == LANE and 2 * d_a == d

    def group_of_tile(rows_per_tile):
        def gid(i):
            r0 = i * rows_per_tile
            return jnp.where(r0 < ctx_rows, 0, 1 + jnp.maximum(r0 - ctx_rows, 0) // seq_l)
        return gid

    gid = group_of_tile(tm)
    gid_c = group_of_tile(tm_c)
    gid_w = group_of_tile(tm_w)

    ng = 1 + nb_l
    ng8 = -(-ng // SUBLANE) * SUBLANE
    cvec = jnp.concatenate([c_ctx[None], c, jnp.zeros((ng8 - ng, d), F32)], axis=0)
    mod = _modulation(cvec, w_mod, b_mod).reshape(depth, ng8, N_MOD, d)


    w_ffn_in_b = w_ffn_in.astype(BF16)
    w_ffn_out_b = w_ffn_out.astype(BF16)

    new_k, new_v, new_sre, new_sim = [], [], [], []
    for l in range(depth):
        g_l = norm_g[l]
        mod_l = mod[l]
        ffn1 = functools.partial(_ffn, mod_l=mod_l, g_l=g_l, w_in=w_ffn_in_b, w_out=w_ffn_out_b, l=l, k=0,
                                 mi=0, gi=0, gid=gid, tm=tm)
        if l > 0:
            x = ffn1(x)
        else:
            x = ffn1(x_prompt.reshape(ctx_rows, d), out_rows=n)
            x = ffn1(x_sample.reshape(n - ctx_rows, d), stream_row0=ctx_rows, out_rows=n, out_row0=ctx_rows,
                     into=x)
        if l % 2 == 0:
            e = l // 2
            z, u, kv = _even_in(x, mod_l, g_l, w_in_even[e].astype(BF16), gid=gid_w, tm=tm_w, du=d_a)
            new_k.append(z[:ctx_rows, 2 * d_a:3 * d_a].reshape(nb_c, seq_c, nh, -1))
            new_v.append(z[:ctx_rows, 3 * d_a:].reshape(nb_c, seq_c, nh, -1))
            kst, kfull, kout, ar, ai = _s5_weights(s5_lam_re[e], s5_lam_im[e], s5_log_dt[e], s5_b_re[e],
                                                   s5_b_im[e], s5_c_re[e], s5_c_im[e], s5_d[e], gb)
            h0_c = jnp.zeros((nb_c, ngrp * 4 * p_len), F32)
            ys_c, hfin = _s5(_to_chunks(u[:ctx_rows], nb_c, seq_c), kst, kfull, kout, ar, ai,
                             h0_c, nseq=nb_c)
            fin_re, fin_im = _cols_to_state(hfin, ngrp, p_len, gb)
            new_sre.append(fin_re)
            new_sim.append(fin_im)
            h0_l = _state_to_cols(state_s5_re[:, e], state_s5_im[:, e], gb)
            ys_l, _ = _s5(_to_chunks(u[ctx_rows:], nb_l, seq_l), kst, kfull, kout, ar, ai,
                          h0_l, nseq=nb_l)
            yb_c = _ctx_attn(z, nbatch=nb_c, seq=seq_c, nh=nh, dcols=d_a)
            bias, win_r = _na_bias(na_rpb[e], seq_l // GRID_W)
            yb_l = _na_attn(z, kv, cache_k[:, e].reshape(nb_l * past, -1).astype(BF16),
                            cache_v[:, e].reshape(nb_l * past, -1).astype(BF16), bias,
                            nbatch=nb_l, seq=seq_l, past=past, nh=nh, row0=ctx_rows, win_r=win_r, dcols=d_a)
            x = _even_out(_from_chunks(ys_c, nb_c, seq_c), _from_chunks(ys_l, nb_l, seq_l), yb_c, yb_l,
                          x, mod_l, g_l, w_glu[e].astype(BF16), w_out_even[e].astype(BF16), gid=gid, tm=tm)
        else:
            o = l // 2
            b, p = _odd_in(x, mod_l, g_l, w_in_conv[o].astype(BF16), gid=gid_w, tm=tm_w)
            x = _odd_out(b, p, conv_w[o], x, mod_l, g_l, w_out_conv[o].astype(BF16), gid=gid_c, tm=tm_c,
                         ctx_rows=ctx_rows, ctx_seq=seq_c, lat_seq=seq_l)
        ffn2 = functools.partial(_ffn, x, mod_l, g_l, w_ffn_in_b, w_ffn_out_b, l=l, k=1,
                                 mi=6, gi=4, gid=gid, tm=tm)
        if l < depth - 1:
            x = ffn2()
        else:
            y_prompt = ffn2(row0=0, nrows=ctx_rows).reshape(nb_c, seq_c, d)
            y_sample = ffn2(row0=ctx_rows, nrows=n - ctx_rows).reshape(nb_l, seq_l, d)

    return (y_prompt, y_sample, jnp.stack(new_k, axis=1), jnp.stack(new_v, axis=1),
            jnp.stack(new_sre, axis=1), jnp.stack(new_sim, axis=1))
```

```python
import functools
import math

import numpy as np
import jax
import jax.numpy as jnp
from jax import lax
from jax.experimental import pallas as pl
from jax.experimental.pallas import tpu as pltpu

F32 = jnp.float32
BF16 = jnp.bfloat16

NORM_EPS = 1e-6
N_MOD = 9
GRID_W = 64
NEG_INF = -1e30
S5_CHUNK = 8
NA_QROWS = 4
NA_HEADS = 4
LANE = 128
SUBLANE = 8
VMEM_LIMIT_BYTES = 56 * 1024 * 1024
HI = lax.Precision.HIGHEST


def _cparams(sem):
    return pltpu.CompilerParams(dimension_semantics=sem, vmem_limit_bytes=VMEM_LIMIT_BYTES)


def _tile(n, pref):
    t = min(n, pref)
    while n % t:
        t //= 2
    return t


def _resident(block_shape, index_map):
    return pl.BlockSpec(block_shape, index_map, pipeline_mode=pl.Buffered(1))


def _rms(x):
    return x * lax.rsqrt(jnp.mean(x * x, axis=-1, keepdims=True) + NORM_EPS)


def _modnorm(x, g, shift, scale):
    return _rms(x) * (g * (1.0 + scale)) + shift


def _post_residual(x, o, g, gate):
    return x + _rms(o) * (gate * g)


def _dot(a, b):
    return jnp.dot(a, b, preferred_element_type=F32)


def _dot_nt(a, b):
    return lax.dot_general(a, b, (((1,), (1,)), ((), ())), preferred_element_type=F32)


def _mod_kernel(c_ref, w_ref, b_ref, o_ref):
    c = c_ref[...]
    s = (c * jax.nn.sigmoid(c)).astype(BF16)
    o_ref[0] = _dot(s, w_ref[0].astype(BF16)) + b_ref[0]


def _modulation(cvec, w_mod, b_mod):
    depth, d, nd = w_mod.shape
    ng = cvec.shape[0]
    tn = _tile(nd, 1024)
    return pl.pallas_call(
        _mod_kernel,
        grid=(depth, nd // tn),
        in_specs=[
            pl.BlockSpec((ng, d), lambda l, j: (0, 0)),
            pl.BlockSpec((1, d, tn), lambda l, j: (l, 0, j)),
            pl.BlockSpec((1, 1, tn), lambda l, j: (l, 0, j)),
        ],
        out_specs=pl.BlockSpec((1, ng, tn), lambda l, j: (l, 0, j)),
        out_shape=jax.ShapeDtypeStruct((depth, ng, nd), F32),
        name="modulation",
        compiler_params=_cparams(("arbitrary", "arbitrary")),
    )(cvec, w_mod, b_mod.reshape(depth, 1, nd))


def _ffn_kernel(x_ref, m_ref, g_ref, wg_ref, wu_ref, wo_ref, *rest, mi, gi, nj):
    o_ref, h_ref = rest[-2:]
    j = pl.program_id(1)

    def step(first, last):
        if first:
            h = _modnorm(x_ref[...], g_ref[gi:gi + 1, :], m_ref[0, mi:mi + 1, :],
                         m_ref[0, mi + 1:mi + 2, :]).astype(BF16)
            h_ref[...] = h
        else:
            h = h_ref[...]
        a = _dot(h, wg_ref[0, 0])
        u = _dot(h, wu_ref[0, 0])
        hid = (a * jax.nn.sigmoid(a) * u).astype(BF16)
        acc = _dot(hid, wo_ref[0, 0])
        if not first:
            acc = o_ref[...] + acc
        if last:
            acc = _post_residual(x_ref[...], acc, g_ref[gi + 1:gi + 2, :], 0.5 * m_ref[0, mi + 2:mi + 3, :])
        o_ref[...] = acc

    if nj == 1:
        step(True, True)
        return
    pl.when(j == 0)(functools.partial(step, True, False))
    if nj > 2:
        pl.when(jnp.logical_and(j > 0, j < nj - 1))(functools.partial(step, False, False))
    pl.when(j == nj - 1)(functools.partial(step, False, True))


def _ffn(x, mod_l, g_l, w_in, w_out, *, l, k, mi, gi, gid, tm, row0=0, nrows=None, stream_row0=None,
         out_rows=None, out_row0=0, into=None):
    d = x.shape[1]
    n = x.shape[0] if nrows is None else nrows
    t0 = row0 // tm
    s0 = t0 if stream_row0 is None else stream_row0 // tm
    o0 = out_row0 // tm
    dff = w_out.shape[2]
    tf = _tile(dff, 512)
    nj = dff // tf
    kern = functools.partial(_ffn_kernel, mi=mi, gi=gi, nj=nj)
    in_specs = [
        pl.BlockSpec((tm, d), lambda i, j: (t0 + i, 0)),
        pl.BlockSpec((1, N_MOD, d), lambda i, j: (gid(s0 + i), 0, 0)),
        pl.BlockSpec(g_l.shape, lambda i, j: (0, 0)),
        pl.BlockSpec((1, 1, d, tf), lambda i, j: (l, k, 0, j)),
        pl.BlockSpec((1, 1, d, tf), lambda i, j: (l, k, 0, j + nj)),
        pl.BlockSpec((1, 1, tf, d), lambda i, j: (l, k, j, 0)),
    ]
    args = [x, mod_l, g_l, w_in, w_in, w_out]
    aliases = {}
    if into is not None:
        aliases = {len(args): 0}
        in_specs.append(pl.BlockSpec(memory_space=pl.ANY))
        args.append(into)
    return pl.pallas_call(
        kern,
        grid=(n // tm, nj),
        in_specs=in_specs,
        out_specs=pl.BlockSpec((tm, d), lambda i, j: (o0 + i, 0)),
        out_shape=jax.ShapeDtypeStruct((n if out_rows is None else out_rows, d), F32),
        scratch_shapes=[pltpu.VMEM((tm, d), BF16)],
        input_output_aliases=aliases,
        name="ffn",
        compiler_params=_cparams(("arbitrary", "arbitrary")),
    )(*args)


def _mixer_in_h(first, x_ref, m_ref, g_ref, h_ref):
    if first:
        h = _modnorm(x_ref[...], g_ref[2:3, :], m_ref[0, 3:4, :], m_ref[0, 4:5, :]).astype(BF16)
        h_ref[...] = h
        return h
    return h_ref[...]


def _even_in_kernel(x_ref, m_ref, g_ref, w_ref, z_ref, u_ref, kv_ref, h_ref):
    def step(first, is_kv):
        z = _dot(_mixer_in_h(first, x_ref, m_ref, g_ref, h_ref), w_ref[...])
        z_ref[...] = z
        if first:
            u_ref[...] = z.astype(u_ref.dtype)
        if is_kv:
            kv_ref[...] = z.astype(kv_ref.dtype)

    j = pl.program_id(1)
    pl.when(j == 0)(functools.partial(step, True, False))
    pl.when(j == 1)(functools.partial(step, False, False))
    pl.when(j >= 2)(functools.partial(step, False, True))


def _even_in(x, mod_l, g_l, w, *, gid, tm, du):
    n, d = x.shape
    nz = w.shape[1]
    tn = du
    assert nz == 4 * du
    return pl.pallas_call(
        _even_in_kernel,
        grid=(n // tm, nz // tn),
        in_specs=[
            pl.BlockSpec((tm, d), lambda i, j: (i, 0)),
            pl.BlockSpec((1, N_MOD, d), lambda i, j: (gid(i), 0, 0)),
            pl.BlockSpec(g_l.shape, lambda i, j: (0, 0)),
            pl.BlockSpec((d, tn), lambda i, j: (0, j)),
        ],
        out_specs=[pl.BlockSpec((tm, tn), lambda i, j: (i, j)),
                   pl.BlockSpec((tm, du), lambda i, j: (i, 0)),
                   pl.BlockSpec((tm, du), lambda i, j: (i, jnp.maximum(j - 2, 0)))],
        out_shape=[jax.ShapeDtypeStruct((n, nz), F32), jax.ShapeDtypeStruct((n, du), BF16),
                   jax.ShapeDtypeStruct((n, 2 * du), BF16)],
        scratch_shapes=[pltpu.VMEM((tm, d), BF16)],
        name="even_in",
        compiler_params=_cparams(("arbitrary", "arbitrary")),
    )(x, mod_l, g_l, w)


def _split_bf16(v):
    hi = v.astype(BF16)
    return hi, (v - hi.astype(F32)).astype(BF16)


def _s5_expand(kstc_ref, koutc_ref, xk_ref, bt_ref, dvec_ref, kst_s, kfull_s, kout_s, *, gb, t_len):
    kdim, w = kst_s.shape
    w2 = w // 2
    dp = w2 // gb
    gh = kdim // t_len
    h_len = gh // gb
    grp_r = (lax.broadcasted_iota(jnp.int32, (kdim, 1), 0) // h_len) % gb
    grp_c = (lax.broadcasted_iota(jnp.int32, (1, kdim), 1) // h_len) % gb

    kc = kstc_ref[0]
    ko = koutc_ref[0].T
    for b in range(gb):
        for half in range(2):
            src = slice(half * dp, (half + 1) * dp)
            dst = slice(half * w2 + b * dp, half * w2 + (b + 1) * dp)
            kst_s[:, dst] = jnp.where(grp_r == b, kc[:, src], 0.0).astype(BF16)
            kout_s[dst, :] = jnp.where(grp_c == b, ko[src, :], 0.0).astype(BF16)

    lane = lax.broadcasted_iota(jnp.int32, (1, 2 * dp), 1)
    fwd_lane = (lane % dp) < (dp // 2)
    same_grp = grp_r[:gh] == grp_c[:, :gh]
    bt = bt_ref[0]
    xh, xl = _split_bf16(xk_ref[0])
    blocks = []
    for lanes in (fwd_lane, jnp.logical_not(fwd_lane)):
        bh, bl = _split_bf16(jnp.where(lanes, bt, 0.0))
        per_lag = []
        for k in range(t_len):
            rk = slice(k * gh, (k + 1) * gh)
            p = _dot_nt(bh, xh[rk]) + _dot_nt(bh, xl[rk]) + _dot_nt(bl, xh[rk])
            per_lag.append(jnp.where(same_grp, p, 0.0))
        blocks.append(per_lag)
    eye = lax.broadcasted_iota(jnp.int32, (gh, gh), 0) == lax.broadcasted_iota(jnp.int32, (gh, gh), 1)
    skip = jnp.where(eye, dvec_ref[0], 0.0)
    for s in range(t_len):
        for t in range(t_len):
            tile = skip if s == t else None
            if t >= s:
                tile = blocks[0][t - s] if tile is None else tile + blocks[0][t - s]
            if s >= t:
                tile = blocks[1][s - t] if tile is None else tile + blocks[1][s - t]
            kfull_s[s * gh:(s + 1) * gh, t * gh:(t + 1) * gh] = tile.astype(BF16)


def _s5_kernel(u_ref, kstc_ref, koutc_ref, xk_ref, bt_ref, dvec_ref, ar_ref, ai_ref, h0_ref, y_ref, hfin_ref,
               s_ref, kst_s, kfull_s, kout_s, *, nseq, nchunk, rt, gb):
    t_len, rows, _ = u_ref.shape
    w = s_ref.shape[2]
    w2 = w // 2
    _s5_expand(kstc_ref, koutc_ref, xk_ref, bt_ref, dvec_ref, kst_s, kfull_s, kout_s, gb=gb, t_len=t_len)

    def chunk_lhs(r0):
        return jnp.concatenate([u_ref[t, r0:r0 + rt, :] for t in range(t_len)], axis=-1)

    for r0 in range(0, rows, rt):
        res = _dot(chunk_lhs(r0), kst_s[...])
        s_ref[r0 // SUBLANE:(r0 + rt) // SUBLANE] = res.reshape(rt // SUBLANE, SUBLANE, w)

    ar = ar_ref[...]
    ai = ai_ref[...]

    def step(hr, hi, sr, si):
        return ar * hr - ai * hi + sr, ar * hi + ai * hr + si

    if nseq % SUBLANE == 0:
        nb = nseq // SUBLANE
        d0 = (lax.broadcasted_iota(jnp.int32, (nb, SUBLANE, w2), 2) & (LANE // 2)) == 0
        hr0 = h0_ref[:, 0:w2].reshape(nb, SUBLANE, w2)
        hi0 = h0_ref[:, w2:w].reshape(nb, SUBLANE, w2)

        def body(k, carry):
            hr, hi = carry
            k1 = nchunk - 1 - k
            b0r = s_ref[pl.ds(k * nb, nb), :, 0:w2]
            b0i = s_ref[pl.ds(k * nb, nb), :, w2:w]
            b1r = s_ref[pl.ds(k1 * nb, nb), :, 0:w2]
            b1i = s_ref[pl.ds(k1 * nb, nb), :, w2:w]
            s_ref[pl.ds(k * nb, nb), :, 0:w2] = jnp.where(d0, hr, b0r)
            s_ref[pl.ds(k * nb, nb), :, w2:w] = jnp.where(d0, hi, b0i)
            s_ref[pl.ds(k1 * nb, nb), :, 0:w2] = jnp.where(d0, b1r, hr)
            s_ref[pl.ds(k1 * nb, nb), :, w2:w] = jnp.where(d0, b1i, hi)
            return step(hr, hi, jnp.where(d0, b0r, b1r), jnp.where(d0, b0i, b1i))

        assert nchunk % 2 == 0
        hr, hi = lax.fori_loop(0, nchunk, body, (hr0, hi0))
        hfin_ref[:, 0:w2] = hr.reshape(nseq, w2)
        hfin_ref[:, w2:w] = hi.reshape(nseq, w2)
    else:
        half = SUBLANE // 2
        assert nseq == half and nchunk % 4 == 0
        nblk = nchunk // 2
        d0 = (lax.broadcasted_iota(jnp.int32, (half, w2), 1) & (LANE // 2)) == 0
        lo = slice(0, half)
        up = slice(half, SUBLANE)

        def body(k, carry):
            hr, hi = carry
            k1 = nblk - 1 - k
            b0r = s_ref[k, :, 0:w2]
            b0i = s_ref[k, :, w2:w]
            b1r = s_ref[k1, :, 0:w2]
            b1i = s_ref[k1, :, w2:w]
            for fa, ba in ((lo, up), (up, lo)):
                s_ref[k, fa, 0:w2] = jnp.where(d0, hr, b0r[fa])
                s_ref[k, fa, w2:w] = jnp.where(d0, hi, b0i[fa])
                s_ref[k1, ba, 0:w2] = jnp.where(d0, b1r[ba], hr)
                s_ref[k1, ba, w2:w] = jnp.where(d0, b1i[ba], hi)
                hr, hi = step(hr, hi, jnp.where(d0, b0r[fa], b1r[ba]), jnp.where(d0, b0i[fa], b1i[ba]))
            return hr, hi

        hr, hi = lax.fori_loop(0, nblk, body, (h0_ref[:, 0:w2], h0_ref[:, w2:w]))
        hfin_ref[:, 0:w2] = hr
        hfin_ref[:, w2:w] = hi

    for r0 in range(0, rows, rt):
        hp = s_ref[r0 // SUBLANE:(r0 + rt) // SUBLANE].reshape(rt, w).astype(BF16)
        y = _dot(chunk_lhs(r0), kfull_s[...]) + _dot(hp, kout_s[...])
        for t in range(t_len):
            y_ref[t, r0:r0 + rt, :] = y[:, t * LANE:(t + 1) * LANE].astype(y_ref.dtype)


def _s5(u_t, factors, ar, ai, h0, *, nseq, gb):
    t_len, rows, _ = u_t.shape
    nblk, kdim, dp2 = factors[0].shape
    sdim = gb * dp2
    nchunk = rows // nseq
    kern = functools.partial(_s5_kernel, nseq=nseq, nchunk=nchunk, rt=_tile(rows, 512), gb=gb)
    state_bytes = rows * sdim * 4
    wspec = pl.BlockSpec if 4 * state_bytes <= VMEM_LIMIT_BYTES else _resident
    return pl.pallas_call(
        kern,
        grid=(nblk,),
        in_specs=[
            pl.BlockSpec((t_len, rows, LANE), lambda i: (0, 0, i)),
        ] + [wspec((1,) + f.shape[1:], lambda i: (i, 0, 0)) for f in factors] + [
            pl.BlockSpec((1, sdim // 2), lambda i: (0, i)),
            pl.BlockSpec((1, sdim // 2), lambda i: (0, i)),
            pl.BlockSpec((nseq, sdim), lambda i: (0, i)),
        ],
        out_specs=[pl.BlockSpec((t_len, rows, LANE), lambda i: (0, 0, i)),
                   pl.BlockSpec((nseq, sdim), lambda i: (0, i))],
        out_shape=[jax.ShapeDtypeStruct(u_t.shape, BF16),
                   jax.ShapeDtypeStruct((nseq, nblk * sdim), F32)],
        scratch_shapes=[pltpu.VMEM((rows // SUBLANE, SUBLANE, sdim), F32),
                        pltpu.VMEM((kdim, sdim), BF16), pltpu.VMEM((kdim, kdim), BF16),
                        pltpu.VMEM((sdim, kdim), BF16)],
        name="s5_seq%d" % nseq,
        compiler_params=_cparams(("arbitrary",)),
    )(u_t, *factors, ar, ai, h0)


def _s5_weights(lam_re, lam_im, log_dt, b_re, b_im, c_re, c_im, d_skip, gb):
    t_len = S5_CHUNK
    _, ngrp, p_len = lam_re.shape
    h_len = b_re.shape[-1]
    dt = jnp.exp(log_dt)[..., None]
    e = jnp.exp(lam_re * dt)
    lbr = e * jnp.cos(lam_im * dt)
    lbi = e * jnp.sin(lam_im * dt)
    den = lam_re * lam_re + lam_im * lam_im
    fr = ((lbr - 1.0) * lam_re + lbi * lam_im) / den
    fi = (lbi * lam_re - (lbr - 1.0) * lam_im) / den
    br = fr[..., None] * b_re - fi[..., None] * b_im
    bi = fr[..., None] * b_im + fi[..., None] * b_re
    pr = [jnp.ones_like(lbr)]
    pi = [jnp.zeros_like(lbr)]
    for _ in range(t_len):
        pr.append(pr[-1] * lbr - pi[-1] * lbi)
        pi.append(pr[-2] * lbi + pi[-1] * lbr)
    pwr = jnp.stack(pr)
    pwi = jnp.stack(pi)

    nblk = ngrp // gb
    dp = 2 * p_len
    tt = np.arange(t_len)

    def rows_of(tab):
        tab = jnp.transpose(tab, (0, 2, 1, 3)).reshape(t_len, nblk, gb, 1, dp)
        return jnp.transpose(tab, (1, 0, 2, 3, 4))

    def chan_of(x, perm):
        return jnp.transpose(x, perm).reshape(nblk, 1, gb, h_len, dp)

    btr, bti = chan_of(br, (1, 3, 0, 2)), chan_of(bi, (1, 3, 0, 2))
    ctr, cti = chan_of(c_re, (1, 2, 0, 3)), chan_of(c_im, (1, 2, 0, 3))

    def compact(tab_r, tab_i, ch_r, ch_i, im_sign):
        tr, ti = rows_of(tab_r), rows_of(tab_i)
        re = tr * ch_r - ti * ch_i
        im = tr * ch_i + ti * ch_r
        return jnp.concatenate([re, im_sign * im], axis=-1).reshape(nblk, t_len * gb * h_len, 2 * dp)

    def per_dir(fwd, bwd):
        return (jnp.stack([pwr[fwd, 0], pwr[bwd, 1]], axis=1), jnp.stack([pwi[fwd, 0], pwi[bwd, 1]], axis=1))

    kst_c = compact(*per_dir(t_len - 1 - tt, tt), btr, bti, 1.0)
    kout_c = compact(*per_dir(tt + 1, t_len - tt), ctr, cti, -1.0)
    xk_c = compact(*per_dir(tt, tt), ctr, cti, 1.0)
    bt_c = jnp.concatenate([btr, -bti], axis=-1).reshape(nblk, gb * h_len, 2 * dp)
    dvec = d_skip.reshape(nblk, 1, gb * h_len)

    ar = jnp.transpose(pwr[t_len], (1, 0, 2)).reshape(1, ngrp * dp)
    ai = jnp.transpose(pwi[t_len], (1, 0, 2)).reshape(1, ngrp * dp)
    return (kst_c, kout_c, xk_c, bt_c, dvec), ar, ai


def _state_to_cols(s_re, s_im, gb):
    b, _, ngrp, p_len = s_re.shape
    def blk(x):
        return jnp.transpose(x, (0, 2, 1, 3)).reshape(b, ngrp // gb, 1, gb * 2 * p_len)
    return jnp.concatenate([blk(s_re), blk(s_im)], axis=2).reshape(b, ngrp * 4 * p_len)


def _cols_to_state(cols, ngrp, p_len, gb):
    b = cols.shape[0]
    x = cols.reshape(b, ngrp // gb, 2, gb, 2, p_len)
    x = jnp.transpose(x, (2, 0, 4, 1, 3, 5)).reshape(2, b, 2, ngrp, p_len)
    return x[0], x[1]


def _to_chunks(u, nseq, seq_len):
    nchunk = seq_len // S5_CHUNK
    x = u.reshape(nseq, nchunk, S5_CHUNK, u.shape[-1])
    return jnp.transpose(x, (2, 1, 0, 3)).reshape(S5_CHUNK, nchunk * nseq, u.shape[-1]).astype(BF16)


def _from_chunks(y, nseq, seq_len):
    nchunk = seq_len // S5_CHUNK
    x = y.reshape(S5_CHUNK, nchunk, nseq, y.shape[-1])
    return jnp.transpose(x, (2, 1, 0, 3)).reshape(nseq * seq_len, y.shape[-1])


def _ctx_attn_kernel(q_ref, k_ref, v_ref, o_ref, *, nh, dh):
    scale = dh ** -0.5
    for h in range(nh):
        sl = slice(h * dh, (h + 1) * dh)
        s = _dot_nt(q_ref[:, sl].astype(BF16), k_ref[:, sl].astype(BF16)) * scale
        e = jnp.exp(s - jnp.max(s, axis=-1, keepdims=True))
        o = _dot(e.astype(BF16), v_ref[:, sl].astype(BF16)) / jnp.sum(e, axis=-1, keepdims=True)
        o_ref[:, sl] = o.astype(o_ref.dtype)


def _ctx_attn(z, *, nbatch, seq, nh, dcols):
    kern = functools.partial(_ctx_attn_kernel, nh=nh, dh=dcols // nh)
    return pl.pallas_call(
        kern,
        grid=(nbatch,),
        in_specs=[pl.BlockSpec((seq, dcols), lambda b, c=c: (b, c)) for c in (1, 2, 3)],
        out_specs=pl.BlockSpec((seq, dcols), lambda b: (b, 0)),
        out_shape=jax.ShapeDtypeStruct((nbatch * seq, dcols), BF16),
        name="ctx_attn",
        compiler_params=_cparams(("arbitrary",)),
    )(z, z, z)


def _na_window_start(i, rows, win_r):
    nkr = win_r + NA_QROWS
    first = jnp.clip(NA_QROWS * i - win_r // 2, 0, rows - win_r)
    return jnp.minimum(first, rows - nkr)


def _na_kernel(q_ref, kb_ref, vb_ref, kc_ref, vc_ref, bias_ref, o_ref, *, rows, win_r, dh, hb):
    i = pl.program_id(2)
    nk = (win_r + NA_QROWS) * GRID_W
    start = pl.multiple_of(_na_window_start(i, rows, win_r) * GRID_W, GRID_W)
    outs = []
    for h in range(hb):
        sl = slice(h * dh, (h + 1) * dh)
        q = (q_ref[:, sl] * (dh ** -0.5)).astype(BF16)
        s_loc = _dot_nt(q, kb_ref[pl.ds(start, nk), sl]) + bias_ref[0, h]
        s_ctx = _dot_nt(q, kc_ref[:, sl])
        m = jnp.maximum(jnp.max(s_loc, axis=-1, keepdims=True), jnp.max(s_ctx, axis=-1, keepdims=True))
        e_loc = jnp.exp(s_loc - m)
        e_ctx = jnp.exp(s_ctx - m)
        den = jnp.sum(e_loc, axis=-1, keepdims=True) + jnp.sum(e_ctx, axis=-1, keepdims=True)
        o = _dot(e_loc.astype(BF16), vb_ref[pl.ds(start, nk), sl]) + _dot(e_ctx.astype(BF16), vc_ref[:, sl])
        outs.append((o / den).astype(o_ref.dtype))
    o_ref[...] = jnp.concatenate(outs, axis=-1)


def _na_bias(rpb, rows):
    nh, nir, nic = rpb.shape
    win_r = (nir + 1) // 2
    win_c = (nic + 1) // 2
    wr = min(win_r, rows)
    nkr = wr + NA_QROWS
    nblk = rows // NA_QROWS
    col = np.arange(GRID_W)
    c0 = np.clip(col - win_c // 2, 0, GRID_W - win_c)
    col_ok = (col[None, :] >= c0[:, None]) & (col[None, :] < c0[:, None] + win_c)
    ic = np.clip(col[None, :] - col[:, None] + (win_c - 1), 0, 2 * win_c - 2)

    def row_pattern(i):
        start = min(int(np.clip(NA_QROWS * i - wr // 2, 0, rows - wr)), rows - nkr)
        r = NA_QROWS * i + np.arange(NA_QROWS)
        kr = start + np.arange(nkr)
        kstart = np.clip(r - wr // 2, 0, rows - wr)
        row_ok = (kr[None, :] >= kstart[:, None]) & (kr[None, :] < kstart[:, None] + wr)
        ir = np.clip(kr[None, :] - r[:, None] + (win_r - 1), 0, nir - 1)
        assert row_ok.sum(axis=1).min() == wr
        return row_ok, np.where(row_ok, ir, 0)

    for i in range(2, nblk - 1):
        assert all(np.array_equal(a, b) for a, b in zip(row_pattern(i), row_pattern(1)))

    sel_c = jnp.asarray((ic[None] == np.arange(nic)[:, None, None]).astype(np.float32))
    by_col = jnp.einsum('hic,cqk->hiqk', rpb.astype(F32), sel_c, precision=HI)
    tables = []
    for i in (0, 1, nblk - 1):
        row_ok, ir = row_pattern(i)
        ok = row_ok[:, None, :, None] & col_ok[None, :, None, :]
        sel_r = jnp.asarray((ir[None] == np.arange(nir)[:, None, None]).astype(np.float32))
        vals = jnp.einsum('hiqk,iaj->haqjk', by_col, sel_r, precision=HI)
        vals = jnp.where(jnp.asarray(ok)[None], vals, NEG_INF)
        tables.append(vals.reshape(nh, NA_QROWS * GRID_W, nkr * GRID_W))
    return jnp.stack(tables), wr


def _na_attn(z, kv, kc, vc, bias, *, nbatch, seq, past, nh, row0, win_r, dcols):
    dh = dcols // nh
    rows = seq // GRID_W
    tq = NA_QROWS * GRID_W
    nblk = rows // NA_QROWS
    assert row0 % seq == 0 and rows % NA_QROWS == 0 and rows >= win_r + NA_QROWS
    hb = _tile(nh, NA_HEADS)
    hw = hb * dh
    ng = nh // hb
    kern = functools.partial(_na_kernel, rows=rows, win_r=win_r, dh=dh, hb=hb)
    qoff = row0 // tq
    boff = row0 // seq

    def cls(i):
        return jnp.where(i == 0, 0, jnp.where(i == nblk - 1, 2, 1))

    c_spec = pl.BlockSpec((past, hw), lambda b, h, i: (b, h))
    return pl.pallas_call(
        kern,
        grid=(nbatch, ng, nblk),
        in_specs=[
            pl.BlockSpec((tq, hw), lambda b, h, i: (qoff + b * nblk + i, ng + h)),
            pl.BlockSpec((seq, hw), lambda b, h, i: (boff + b, h)),
            pl.BlockSpec((seq, hw), lambda b, h, i: (boff + b, ng + h)),
            c_spec, c_spec,
            pl.BlockSpec((1, hb) + bias.shape[2:], lambda b, h, i: (cls(i), h, 0, 0)),
        ],
        out_specs=pl.BlockSpec((tq, hw), lambda b, h, i: (b * nblk + i, h)),
        out_shape=jax.ShapeDtypeStruct((nbatch * seq, dcols), BF16),
        name="na_attn",
        compiler_params=_cparams(("arbitrary", "arbitrary", "arbitrary")),
    )(z, kv, kv, kc, vc, bias)


def _even_out_kernel(ysc_ref, ysl_ref, ybc_ref, ybl_ref, x_ref, m_ref, g_ref, wglu_ref, woa_ref, wob_ref,
                     o_ref, *, nc):
    is_ctx = pl.program_id(0) < nc
    ys = jnp.where(is_ctx, ysc_ref[...], ysl_ref[...])
    yb = jnp.where(is_ctx, ybc_ref[...], ybl_ref[...])
    y = jax.nn.gelu(ys.astype(F32))
    gate = jax.nn.sigmoid(_dot(y.astype(BF16), wglu_ref[...]))
    ya = (y * gate).astype(BF16)
    o = _dot(ya, woa_ref[...]) + _dot(yb, wob_ref[...])
    o_ref[...] = _post_residual(x_ref[...], o, g_ref[3:4, :], m_ref[0, 5:6, :])


def _even_out(ys_c, ys_l, yb_c, yb_l, x, mod_l, g_l, w_glu, w_out, *, gid, tm):
    n, d = x.shape
    da = ys_c.shape[1]
    db = yb_c.shape[1]
    nc = ys_c.shape[0] // tm
    assert da == db and nc * tm == ys_c.shape[0] == yb_c.shape[0]

    def ctx_blk(i):
        return (jnp.minimum(i, nc - 1), 0)

    def lat_blk(i):
        return (jnp.maximum(i - nc, 0), 0)

    return pl.pallas_call(
        functools.partial(_even_out_kernel, nc=nc),
        grid=(n // tm,),
        in_specs=[
            pl.BlockSpec((tm, da), ctx_blk),
            pl.BlockSpec((tm, da), lat_blk),
            pl.BlockSpec((tm, db), ctx_blk),
            pl.BlockSpec((tm, db), lat_blk),
            pl.BlockSpec((tm, d), lambda i: (i, 0)),
            pl.BlockSpec((1, N_MOD, d), lambda i: (gid(i), 0, 0)),
            pl.BlockSpec(g_l.shape, lambda i: (0, 0)),
            _resident((da, da), lambda i: (0, 0)),
            _resident((da, d), lambda i: (0, 0)),
            _resident((db, d), lambda i: (1, 0)),
        ],
        out_specs=pl.BlockSpec((tm, d), lambda i: (i, 0)),
        out_shape=jax.ShapeDtypeStruct((n, d), F32),
        name="even_out",
        compiler_params=_cparams(("arbitrary",)),
    )(ys_c, ys_l, yb_c, yb_l, x, mod_l, g_l, w_glu, w_out, w_out)


def _odd_in_kernel(x_ref, m_ref, g_ref, wb_ref, wc_ref, wx_ref, b_ref, p_ref, h_ref):
    def step(first):
        h = _mixer_in_h(first, x_ref, m_ref, g_ref, h_ref)
        b_ref[...] = _dot(h, wb_ref[...])
        p_ref[...] = _dot(h, wc_ref[...]) * _dot(h, wx_ref[...])

    j = pl.program_id(1)
    pl.when(j == 0)(functools.partial(step, True))
    pl.when(j > 0)(functools.partial(step, False))


def _odd_in(x, mod_l, g_l, w, *, gid, tm):
    n, d = x.shape
    dc = w.shape[1] // 3
    tn = _tile(dc, 512)
    nj = dc // tn
    blk = pl.BlockSpec((tm, tn), lambda i, j: (i, j))
    return pl.pallas_call(
        _odd_in_kernel,
        grid=(n // tm, nj),
        in_specs=[
            pl.BlockSpec((tm, d), lambda i, j: (i, 0)),
            pl.BlockSpec((1, N_MOD, d), lambda i, j: (gid(i), 0, 0)),
            pl.BlockSpec(g_l.shape, lambda i, j: (0, 0)),
            pl.BlockSpec((d, tn), lambda i, j: (0, j)),
            pl.BlockSpec((d, tn), lambda i, j: (0, j + nj)),
            pl.BlockSpec((d, tn), lambda i, j: (0, j + 2 * nj)),
        ],
        out_specs=[blk, blk],
        out_shape=[jax.ShapeDtypeStruct((n, dc), F32), jax.ShapeDtypeStruct((n, dc), F32)],
        scratch_shapes=[pltpu.VMEM((tm, d), BF16)],
        name="odd_in",
        compiler_params=_cparams(("arbitrary", "arbitrary")),
    )(x, mod_l, g_l, w, w, w)


def _odd_out_kernel(b_ref, p_ref, pp_ref, pn_ref, cw_ref, x_ref, m_ref, g_ref, w_ref, o_ref,
                    *, tm, ctx_rows, ctx_seq, lat_seq):
    i = pl.program_id(0)
    p = p_ref[...]
    row = lax.broadcasted_iota(jnp.int32, (tm, 1), 0)
    grow = i * tm + row
    is_ctx = grow < ctx_rows
    pos = jnp.where(is_ctx, grow % ctx_seq, (grow - ctx_rows) % lat_seq)
    last = jnp.where(is_ctx, ctx_seq - 1, lat_seq - 1)
    prev = pltpu.roll(p, 1, axis=0)
    prev = jnp.where(row == 0, pp_ref[SUBLANE - 1:SUBLANE, :], prev)
    prev = jnp.where(pos == 0, 0.0, prev)
    nxt = pltpu.roll(p, tm - 1, axis=0)
    nxt = jnp.where(row == tm - 1, pn_ref[0:1, :], nxt)
    nxt = jnp.where(pos == last, 0.0, nxt)
    y = cw_ref[0:1, :] * prev + cw_ref[1:2, :] * p + cw_ref[2:3, :] * nxt
    v = (b_ref[...] * y).astype(BF16)
    o = _dot(v, w_ref[...])
    o_ref[...] = _post_residual(x_ref[...], o, g_ref[3:4, :], m_ref[0, 5:6, :])


def _odd_out(b, p, conv_w, x, mod_l, g_l, w, *, gid, tm, ctx_rows, ctx_seq, lat_seq):
    n, d = x.shape
    dc = b.shape[1]
    assert conv_w.shape[0] == 3
    kern = functools.partial(_odd_out_kernel, tm=tm, ctx_rows=ctx_rows, ctx_seq=ctx_seq, lat_seq=lat_seq)
    tpb = tm // SUBLANE
    nb8 = n // SUBLANE
    return pl.pallas_call(
        kern,
        grid=(n // tm,),
        in_specs=[
            pl.BlockSpec((tm, dc), lambda i: (i, 0)),
            pl.BlockSpec((tm, dc), lambda i: (i, 0)),
            pl.BlockSpec((SUBLANE, dc), lambda i: (jnp.maximum(i * tpb - 1, 0), 0)),
            pl.BlockSpec((SUBLANE, dc), lambda i: (jnp.minimum((i + 1) * tpb, nb8 - 1), 0)),
            pl.BlockSpec(conv_w.shape, lambda i: (0, 0)),
            pl.BlockSpec((tm, d), lambda i: (i, 0)),
            pl.BlockSpec((1, N_MOD, d), lambda i: (gid(i), 0, 0)),
            pl.BlockSpec(g_l.shape, lambda i: (0, 0)),
            _resident((dc, d), lambda i: (0, 0)),
        ],
        out_specs=pl.BlockSpec((tm, d), lambda i: (i, 0)),
        out_shape=jax.ShapeDtypeStruct((n, d), F32),
        name="odd_out",
        compiler_params=_cparams(("arbitrary",)),
    )(b, p, p, p, conv_w, x, mod_l, g_l, w)


def kernel(x_prompt, x_sample, cache_k, cache_v, state_s5_re, state_s5_im, c, c_ctx, norm_g, w_mod, b_mod,
           w_ffn_in, w_ffn_out, w_in_even, w_out_even, s5_lam_re, s5_lam_im, s5_log_dt, s5_b_re, s5_b_im,
           s5_c_re, s5_c_im, s5_d, w_glu, na_rpb, w_in_conv, conv_w, w_out_conv):
    nb_c, seq_c, d = x_prompt.shape
    nb_l, seq_l, _ = x_sample.shape
    depth = norm_g.shape[0]
    past, nh = cache_k.shape[2], cache_k.shape[3]
    ngrp, p_len = state_s5_re.shape[3], state_s5_re.shape[4]
    h_len = s5_b_re.shape[-1]
    d_a = ngrp * h_len
    ctx_rows = nb_c * seq_c
    n = ctx_rows + nb_l * seq_l
    tm_w = _tile(math.gcd(ctx_rows, seq_l), 1024)
    tm = _tile(tm_w, 512)
    tm_c = _tile(tm, 256)
    gb = LANE // h_len
    assert gb * h_len == LANE and ngrp % gb == 0 and 2 * p_len == LANE and 2 * d_a == d

    def group_of_tile(rows_per_tile):
        def gid(i):
            r0 = i * rows_per_tile
            return jnp.where(r0 < ctx_rows, 0, 1 + jnp.maximum(r0 - ctx_rows, 0) // seq_l)
        return gid

    gid = group_of_tile(tm)
    gid_c = group_of_tile(tm_c)
    gid_w = group_of_tile(tm_w)

    ng = 1 + nb_l
    ng8 = -(-ng // SUBLANE) * SUBLANE
    cvec = jnp.concatenate([c_ctx[None], c, jnp.zeros((ng8 - ng, d), F32)], axis=0)
    mod = _modulation(cvec, w_mod, b_mod).reshape(depth, ng8, N_MOD, d)


    w_ffn_in_b = w_ffn_in.astype(BF16)
    w_ffn_out_b = w_ffn_out.astype(BF16)

    new_k, new_v, new_sre, new_sim = [], [], [], []
    for l in range(depth):
        g_l = norm_g[l]
        mod_l = mod[l]
        ffn1 = functools.partial(_ffn, mod_l=mod_l, g_l=g_l, w_in=w_ffn_in_b, w_out=w_ffn_out_b, l=l, k=0,
                                 mi=0, gi=0, gid=gid, tm=tm)
        if l > 0:
            x = ffn1(x)
        else:
            x = ffn1(x_prompt.reshape(ctx_rows, d), out_rows=n)
            x = ffn1(x_sample.reshape(n - ctx_rows, d), stream_row0=ctx_rows, out_rows=n, out_row0=ctx_rows,
                     into=x)
        if l % 2 == 0:
            e = l // 2
            z, u, kv = _even_in(x, mod_l, g_l, w_in_even[e].astype(BF16), gid=gid_w, tm=tm_w, du=d_a)
            new_k.append(z[:ctx_rows, 2 * d_a:3 * d_a].reshape(nb_c, seq_c, nh, -1))
            new_v.append(z[:ctx_rows, 3 * d_a:].reshape(nb_c, seq_c, nh, -1))
            factors, ar, ai = _s5_weights(s5_lam_re[e], s5_lam_im[e], s5_log_dt[e], s5_b_re[e],
                                          s5_b_im[e], s5_c_re[e], s5_c_im[e], s5_d[e], gb)
            h0_c = jnp.zeros((nb_c, ngrp * 4 * p_len), F32)
            ys_c, hfin = _s5(_to_chunks(u[:ctx_rows], nb_c, seq_c), factors, ar, ai, h0_c, nseq=nb_c, gb=gb)
            fin_re, fin_im = _cols_to_state(hfin, ngrp, p_len, gb)
            new_sre.append(fin_re)
            new_sim.append(fin_im)
            h0_l = _state_to_cols(state_s5_re[:, e], state_s5_im[:, e], gb)
            ys_l, _ = _s5(_to_chunks(u[ctx_rows:], nb_l, seq_l), factors, ar, ai, h0_l, nseq=nb_l, gb=gb)
            yb_c = _ctx_attn(z, nbatch=nb_c, seq=seq_c, nh=nh, dcols=d_a)
            bias, win_r = _na_bias(na_rpb[e], seq_l // GRID_W)
            yb_l = _na_attn(z, kv, cache_k[:, e].reshape(nb_l * past, -1).astype(BF16),
                            cache_v[:, e].reshape(nb_l * past, -1).astype(BF16), bias,
                            nbatch=nb_l, seq=seq_l, past=past, nh=nh, row0=ctx_rows, win_r=win_r, dcols=d_a)
            x = _even_out(_from_chunks(ys_c, nb_c, seq_c), _from_chunks(ys_l, nb_l, seq_l), yb_c, yb_l,
                          x, mod_l, g_l, w_glu[e].astype(BF16), w_out_even[e].astype(BF16), gid=gid, tm=tm)
        else:
            o = l // 2
            b, p = _odd_in(x, mod_l, g_l, w_in_conv[o].astype(BF16), gid=gid_w, tm=tm_w)
            x = _odd_out(b, p, conv_w[o], x, mod_l, g_l, w_out_conv[o].astype(BF16), gid=gid_c, tm=tm_c,
                         ctx_rows=ctx_rows, ctx_seq=seq_c, lat_seq=seq_l)
        ffn2 = functools.partial(_ffn, x, mod_l, g_l, w_ffn_in_b, w_ffn_out_b, l=l, k=1,
                                 mi=6, gi=4, gid=gid, tm=tm)
        if l < depth - 1:
            x = ffn2()
        else:
            y_prompt = ffn2(row0=0, nrows=ctx_rows).reshape(nb_c, seq_c, d)
            y_sample = ffn2(row0=ctx_rows, nrows=n - ctx_rows).reshape(nb_l, seq_l, d)

    return (y_prompt, y_sample, jnp.stack(new_k, axis=1), jnp.stack(new_v, axis=1),
            jnp.stack(new_sre, axis=1), jnp.stack(new_sim, axis=1))
```

```python
import functools
import math

import numpy as np
import jax
import jax.numpy as jnp
from jax import lax
from jax.experimental import pallas as pl
from jax.experimental.pallas import tpu as pltpu

F32 = jnp.float32
BF16 = jnp.bfloat16

NORM_EPS = 1e-6
N_MOD = 9
GRID_W = 64
NEG_INF = -1e30
S5_CHUNK = 8
NA_QROWS = 4
NA_HEADS = 4
LANE = 128
SUBLANE = 8
VMEM_LIMIT_BYTES = 56 * 1024 * 1024
HI = lax.Precision.HIGHEST


def _cparams(sem):
    return pltpu.CompilerParams(dimension_semantics=sem, vmem_limit_bytes=VMEM_LIMIT_BYTES)


def _tile(n, pref):
    t = min(n, pref)
    while n % t:
        t //= 2
    return t


def _resident(block_shape, index_map):
    return pl.BlockSpec(block_shape, index_map, pipeline_mode=pl.Buffered(1))


def _rms(x):
    return x * lax.rsqrt(jnp.mean(x * x, axis=-1, keepdims=True) + NORM_EPS)


def _modnorm(x, g, shift, scale):
    return _rms(x) * (g * (1.0 + scale)) + shift


def _post_residual(x, o, g, gate):
    return x + _rms(o) * (gate * g)


def _dot(a, b):
    return jnp.dot(a, b, preferred_element_type=F32)


def _dot_nt(a, b):
    return lax.dot_general(a, b, (((1,), (1,)), ((), ())), preferred_element_type=F32)


def _mod_kernel(c_ref, w_ref, b_ref, o_ref):
    c = c_ref[...]
    s = (c * jax.nn.sigmoid(c)).astype(BF16)
    o_ref[0] = _dot(s, w_ref[0].astype(BF16)) + b_ref[0]


def _modulation(cvec, w_mod, b_mod):
    depth, d, nd = w_mod.shape
    ng = cvec.shape[0]
    tn = _tile(nd, 1024)
    return pl.pallas_call(
        _mod_kernel,
        grid=(depth, nd // tn),
        in_specs=[
            pl.BlockSpec((ng, d), lambda l, j: (0, 0)),
            pl.BlockSpec((1, d, tn), lambda l, j: (l, 0, j)),
            pl.BlockSpec((1, 1, tn), lambda l, j: (l, 0, j)),
        ],
        out_specs=pl.BlockSpec((1, ng, tn), lambda l, j: (l, 0, j)),
        out_shape=jax.ShapeDtypeStruct((depth, ng, nd), F32),
        name="modulation",
        compiler_params=_cparams(("arbitrary", "arbitrary")),
    )(cvec, w_mod, b_mod.reshape(depth, 1, nd))


def _ffn_kernel(x_ref, m_ref, g_ref, wg_ref, wu_ref, wo_ref, *rest, mi, gi, nj):
    o_ref, h_ref = rest[-2:]
    j = pl.program_id(1)

    def step(first, last):
        if first:
            h = _modnorm(x_ref[...], g_ref[gi:gi + 1, :], m_ref[0, mi:mi + 1, :],
                         m_ref[0, mi + 1:mi + 2, :]).astype(BF16)
            h_ref[...] = h
        else:
            h = h_ref[...]
        a = _dot(h, wg_ref[0, 0])
        u = _dot(h, wu_ref[0, 0])
        hid = (a * jax.nn.sigmoid(a) * u).astype(BF16)
        acc = _dot(hid, wo_ref[0, 0])
        if not first:
            acc = o_ref[...] + acc
        if last:
            acc = _post_residual(x_ref[...], acc, g_ref[gi + 1:gi + 2, :], 0.5 * m_ref[0, mi + 2:mi + 3, :])
        o_ref[...] = acc

    if nj == 1:
        step(True, True)
        return
    pl.when(j == 0)(functools.partial(step, True, False))
    if nj > 2:
        pl.when(jnp.logical_and(j > 0, j < nj - 1))(functools.partial(step, False, False))
    pl.when(j == nj - 1)(functools.partial(step, False, True))


def _ffn(x, mod_l, g_l, w_in, w_out, *, l, k, mi, gi, gid, tm, row0=0, nrows=None, stream_row0=None,
         out_rows=None, out_row0=0, into=None):
    d = x.shape[1]
    n = x.shape[0] if nrows is None else nrows
    t0 = row0 // tm
    s0 = t0 if stream_row0 is None else stream_row0 // tm
    o0 = out_row0 // tm
    dff = w_out.shape[2]
    tf = _tile(dff, 512)
    nj = dff // tf
    kern = functools.partial(_ffn_kernel, mi=mi, gi=gi, nj=nj)
    in_specs = [
        pl.BlockSpec((tm, d), lambda i, j: (t0 + i, 0)),
        pl.BlockSpec((1, N_MOD, d), lambda i, j: (gid(s0 + i), 0, 0)),
        pl.BlockSpec(g_l.shape, lambda i, j: (0, 0)),
        pl.BlockSpec((1, 1, d, tf), lambda i, j: (l, k, 0, j)),
        pl.BlockSpec((1, 1, d, tf), lambda i, j: (l, k, 0, j + nj)),
        pl.BlockSpec((1, 1, tf, d), lambda i, j: (l, k, j, 0)),
    ]
    args = [x, mod_l, g_l, w_in, w_in, w_out]
    aliases = {}
    if into is not None:
        aliases = {len(args): 0}
        in_specs.append(pl.BlockSpec(memory_space=pl.ANY))
        args.append(into)
    return pl.pallas_call(
        kern,
        grid=(n // tm, nj),
        in_specs=in_specs,
        out_specs=pl.BlockSpec((tm, d), lambda i, j: (o0 + i, 0)),
        out_shape=jax.ShapeDtypeStruct((n if out_rows is None else out_rows, d), F32),
        scratch_shapes=[pltpu.VMEM((tm, d), BF16)],
        input_output_aliases=aliases,
        name="ffn",
        compiler_params=_cparams(("arbitrary", "arbitrary")),
    )(*args)


def _mixer_in_h(first, x_ref, m_ref, g_ref, h_ref):
    if first:
        h = _modnorm(x_ref[...], g_ref[2:3, :], m_ref[0, 3:4, :], m_ref[0, 4:5, :]).astype(BF16)
        h_ref[...] = h
        return h
    return h_ref[...]


def _even_in_kernel(x_ref, m_ref, g_ref, w_ref, z_ref, u_ref, kv_ref, h_ref):
    def step(first, is_kv):
        z = _dot(_mixer_in_h(first, x_ref, m_ref, g_ref, h_ref), w_ref[...])
        z_ref[...] = z
        if first:
            u_ref[...] = z.astype(u_ref.dtype)
        if is_kv:
            kv_ref[...] = z.astype(kv_ref.dtype)

    j = pl.program_id(1)
    pl.when(j == 0)(functools.partial(step, True, False))
    pl.when(j == 1)(functools.partial(step, False, False))
    pl.when(j >= 2)(functools.partial(step, False, True))


def _even_in(x, mod_l, g_l, w, *, gid, tm, du):
    n, d = x.shape
    nz = w.shape[1]
    tn = du
    assert nz == 4 * du
    return pl.pallas_call(
        _even_in_kernel,
        grid=(n // tm, nz // tn),
        in_specs=[
            pl.BlockSpec((tm, d), lambda i, j: (i, 0)),
            pl.BlockSpec((1, N_MOD, d), lambda i, j: (gid(i), 0, 0)),
            pl.BlockSpec(g_l.shape, lambda i, j: (0, 0)),
            pl.BlockSpec((d, tn), lambda i, j: (0, j)),
        ],
        out_specs=[pl.BlockSpec((tm, tn), lambda i, j: (i, j)),
                   pl.BlockSpec((tm, du), lambda i, j: (i, 0)),
                   pl.BlockSpec((tm, du), lambda i, j: (i, jnp.maximum(j - 2, 0)))],
        out_shape=[jax.ShapeDtypeStruct((n, nz), F32), jax.ShapeDtypeStruct((n, du), BF16),
                   jax.ShapeDtypeStruct((n, 2 * du), BF16)],
        scratch_shapes=[pltpu.VMEM((tm, d), BF16)],
        name="even_in",
        compiler_params=_cparams(("arbitrary", "arbitrary")),
    )(x, mod_l, g_l, w)


def _split_bf16(v):
    hi = v.astype(BF16)
    return hi, (v - hi.astype(F32)).astype(BF16)


def _s5_expand(kstc_ref, koutc_ref, xk_ref, bt_ref, dvec_ref, kst_s, kfull_s, kout_s, *, gb, t_len):
    kdim, w = kst_s.shape
    w2 = w // 2
    dp = w2 // gb
    gh = kdim // t_len
    h_len = gh // gb
    grp_r = (lax.broadcasted_iota(jnp.int32, (kdim, 1), 0) // h_len) % gb
    grp_c = (lax.broadcasted_iota(jnp.int32, (1, kdim), 1) // h_len) % gb

    kc = kstc_ref[0]
    ko = koutc_ref[0].T
    for b in range(gb):
        for half in range(2):
            src = slice(half * dp, (half + 1) * dp)
            dst = slice(half * w2 + b * dp, half * w2 + (b + 1) * dp)
            kst_s[:, dst] = jnp.where(grp_r == b, kc[:, src], 0.0).astype(BF16)
            kout_s[dst, :] = jnp.where(grp_c == b, ko[src, :], 0.0).astype(BF16)

    lane = lax.broadcasted_iota(jnp.int32, (1, 2 * dp), 1)
    fwd_lane = (lane % dp) < (dp // 2)
    same_grp = grp_r[:gh] == grp_c[:, :gh]
    bt = bt_ref[0]
    xh, xl = _split_bf16(xk_ref[0])
    blocks = []
    for lanes in (fwd_lane, jnp.logical_not(fwd_lane)):
        bh, bl = _split_bf16(jnp.where(lanes, bt, 0.0))
        per_lag = []
        for k in range(t_len):
            rk = slice(k * gh, (k + 1) * gh)
            p = _dot_nt(bh, xh[rk]) + _dot_nt(bh, xl[rk]) + _dot_nt(bl, xh[rk])
            per_lag.append(jnp.where(same_grp, p, 0.0))
        blocks.append(per_lag)
    eye = lax.broadcasted_iota(jnp.int32, (gh, gh), 0) == lax.broadcasted_iota(jnp.int32, (gh, gh), 1)
    skip = jnp.where(eye, dvec_ref[0], 0.0)
    for s in range(t_len):
        for t in range(t_len):
            tile = skip if s == t else None
            if t >= s:
                tile = blocks[0][t - s] if tile is None else tile + blocks[0][t - s]
            if s >= t:
                tile = blocks[1][s - t] if tile is None else tile + blocks[1][s - t]
            kfull_s[s * gh:(s + 1) * gh, t * gh:(t + 1) * gh] = tile.astype(BF16)


def _s5_kernel(u_ref, kstc_ref, koutc_ref, xk_ref, bt_ref, dvec_ref, ar_ref, ai_ref, h0_ref, y_ref, hfin_ref,
               s_ref, kst_s, kfull_s, kout_s, *, nseq, nchunk, rt, gb):
    t_len, rows, _ = u_ref.shape
    w = s_ref.shape[2]
    w2 = w // 2
    _s5_expand(kstc_ref, koutc_ref, xk_ref, bt_ref, dvec_ref, kst_s, kfull_s, kout_s, gb=gb, t_len=t_len)

    def chunk_lhs(r0):
        return jnp.concatenate([u_ref[t, r0:r0 + rt, :] for t in range(t_len)], axis=-1)

    for r0 in range(0, rows, rt):
        res = _dot(chunk_lhs(r0), kst_s[...])
        s_ref[r0 // SUBLANE:(r0 + rt) // SUBLANE] = res.reshape(rt // SUBLANE, SUBLANE, w)

    ar = ar_ref[...]
    ai = ai_ref[...]

    def step(hr, hi, sr, si):
        return ar * hr - ai * hi + sr, ar * hi + ai * hr + si

    if nseq % SUBLANE == 0:
        nb = nseq // SUBLANE
        d0 = (lax.broadcasted_iota(jnp.int32, (nb, SUBLANE, w2), 2) & (LANE // 2)) == 0
        hr0 = h0_ref[:, 0:w2].reshape(nb, SUBLANE, w2)
        hi0 = h0_ref[:, w2:w].reshape(nb, SUBLANE, w2)

        def body(k, carry):
            hr, hi = carry
            k1 = nchunk - 1 - k
            b0r = s_ref[pl.ds(k * nb, nb), :, 0:w2]
            b0i = s_ref[pl.ds(k * nb, nb), :, w2:w]
            b1r = s_ref[pl.ds(k1 * nb, nb), :, 0:w2]
            b1i = s_ref[pl.ds(k1 * nb, nb), :, w2:w]
            s_ref[pl.ds(k * nb, nb), :, 0:w2] = jnp.where(d0, hr, b0r)
            s_ref[pl.ds(k * nb, nb), :, w2:w] = jnp.where(d0, hi, b0i)
            s_ref[pl.ds(k1 * nb, nb), :, 0:w2] = jnp.where(d0, b1r, hr)
            s_ref[pl.ds(k1 * nb, nb), :, w2:w] = jnp.where(d0, b1i, hi)
            return step(hr, hi, jnp.where(d0, b0r, b1r), jnp.where(d0, b0i, b1i))

        assert nchunk % 2 == 0
        hr, hi = lax.fori_loop(0, nchunk, body, (hr0, hi0))
        hfin_ref[:, 0:w2] = hr.reshape(nseq, w2)
        hfin_ref[:, w2:w] = hi.reshape(nseq, w2)
    else:
        half = SUBLANE // 2
        assert nseq == half and nchunk % 4 == 0
        nblk = nchunk // 2
        d0 = (lax.broadcasted_iota(jnp.int32, (half, w2), 1) & (LANE // 2)) == 0
        lo = slice(0, half)
        up = slice(half, SUBLANE)

        def body(k, carry):
            hr, hi = carry
            k1 = nblk - 1 - k
            b0r = s_ref[k, :, 0:w2]
            b0i = s_ref[k, :, w2:w]
            b1r = s_ref[k1, :, 0:w2]
            b1i = s_ref[k1, :, w2:w]
            for fa, ba in ((lo, up), (up, lo)):
                s_ref[k, fa, 0:w2] = jnp.where(d0, hr, b0r[fa])
                s_ref[k, fa, w2:w] = jnp.where(d0, hi, b0i[fa])
                s_ref[k1, ba, 0:w2] = jnp.where(d0, b1r[ba], hr)
                s_ref[k1, ba, w2:w] = jnp.where(d0, b1i[ba], hi)
                hr, hi = step(hr, hi, jnp.where(d0, b0r[fa], b1r[ba]), jnp.where(d0, b0i[fa], b1i[ba]))
            return hr, hi

        hr, hi = lax.fori_loop(0, nblk, body, (h0_ref[:, 0:w2], h0_ref[:, w2:w]))
        hfin_ref[:, 0:w2] = hr
        hfin_ref[:, w2:w] = hi

    for r0 in range(0, rows, rt):
        hp = s_ref[r0 // SUBLANE:(r0 + rt) // SUBLANE].reshape(rt, w).astype(BF16)
        y = _dot(chunk_lhs(r0), kfull_s[...]) + _dot(hp, kout_s[...])
        for t in range(t_len):
            y_ref[t, r0:r0 + rt, :] = y[:, t * LANE:(t + 1) * LANE].astype(y_ref.dtype)


def _s5(u_t, factors, ar, ai, h0, *, nseq, gb):
    t_len, rows, _ = u_t.shape
    nblk, kdim, dp2 = factors[0].shape
    sdim = gb * dp2
    nchunk = rows // nseq
    kern = functools.partial(_s5_kernel, nseq=nseq, nchunk=nchunk, rt=_tile(rows, 512), gb=gb)
    state_bytes = rows * sdim * 4
    wspec = pl.BlockSpec if 4 * state_bytes <= VMEM_LIMIT_BYTES else _resident
    return pl.pallas_call(
        kern,
        grid=(nblk,),
        in_specs=[
            pl.BlockSpec((t_len, rows, LANE), lambda i: (0, 0, i)),
        ] + [wspec((1,) + f.shape[1:], lambda i: (i, 0, 0)) for f in factors] + [
            pl.BlockSpec((1, sdim // 2), lambda i: (0, i)),
            pl.BlockSpec((1, sdim // 2), lambda i: (0, i)),
            pl.BlockSpec((nseq, sdim), lambda i: (0, i)),
        ],
        out_specs=[pl.BlockSpec((t_len, rows, LANE), lambda i: (0, 0, i)),
                   pl.BlockSpec((nseq, sdim), lambda i: (0, i))],
        out_shape=[jax.ShapeDtypeStruct(u_t.shape, BF16),
                   jax.ShapeDtypeStruct((nseq, nblk * sdim), F32)],
        scratch_shapes=[pltpu.VMEM((rows // SUBLANE, SUBLANE, sdim), F32),
                        pltpu.VMEM((kdim, sdim), BF16), pltpu.VMEM((kdim, kdim), BF16),
                        pltpu.VMEM((sdim, kdim), BF16)],
        name="s5_seq%d" % nseq,
        compiler_params=_cparams(("arbitrary",)),
    )(u_t, *factors, ar, ai, h0)


def _s5_weights(lam_re, lam_im, log_dt, b_re, b_im, c_re, c_im, d_skip, gb):
    t_len = S5_CHUNK
    _, ngrp, p_len = lam_re.shape
    h_len = b_re.shape[-1]
    dt = jnp.exp(log_dt)[..., None]
    e = jnp.exp(lam_re * dt)
    lbr = e * jnp.cos(lam_im * dt)
    lbi = e * jnp.sin(lam_im * dt)
    den = lam_re * lam_re + lam_im * lam_im
    fr = ((lbr - 1.0) * lam_re + lbi * lam_im) / den
    fi = (lbi * lam_re - (lbr - 1.0) * lam_im) / den
    br = fr[..., None] * b_re - fi[..., None] * b_im
    bi = fr[..., None] * b_im + fi[..., None] * b_re
    pr = [jnp.ones_like(lbr)]
    pi = [jnp.zeros_like(lbr)]
    for _ in range(t_len):
        pr.append(pr[-1] * lbr - pi[-1] * lbi)
        pi.append(pr[-2] * lbi + pi[-1] * lbr)
    pwr = jnp.stack(pr)
    pwi = jnp.stack(pi)

    nblk = ngrp // gb
    dp = 2 * p_len
    tt = np.arange(t_len)

    def rows_of(tab):
        tab = jnp.transpose(tab, (0, 2, 1, 3)).reshape(t_len, nblk, gb, 1, dp)
        return jnp.transpose(tab, (1, 0, 2, 3, 4))

    def chan_of(x, perm):
        return jnp.transpose(x, perm).reshape(nblk, 1, gb, h_len, dp)

    btr, bti = chan_of(br, (1, 3, 0, 2)), chan_of(bi, (1, 3, 0, 2))
    ctr, cti = chan_of(c_re, (1, 2, 0, 3)), chan_of(c_im, (1, 2, 0, 3))

    def compact(tab_r, tab_i, ch_r, ch_i, im_sign):
        tr, ti = rows_of(tab_r), rows_of(tab_i)
        re = tr * ch_r - ti * ch_i
        im = tr * ch_i + ti * ch_r
        return jnp.concatenate([re, im_sign * im], axis=-1).reshape(nblk, t_len * gb * h_len, 2 * dp)

    def per_dir(fwd, bwd):
        return (jnp.stack([pwr[fwd, 0], pwr[bwd, 1]], axis=1), jnp.stack([pwi[fwd, 0], pwi[bwd, 1]], axis=1))

    kst_c = compact(*per_dir(t_len - 1 - tt, tt), btr, bti, 1.0)
    kout_c = compact(*per_dir(tt + 1, t_len - tt), ctr, cti, -1.0)
    xk_c = compact(*per_dir(tt, tt), ctr, cti, 1.0)
    bt_c = jnp.concatenate([btr, -bti], axis=-1).reshape(nblk, gb * h_len, 2 * dp)
    dvec = d_skip.reshape(nblk, 1, gb * h_len)

    ar = jnp.transpose(pwr[t_len], (1, 0, 2)).reshape(1, ngrp * dp)
    ai = jnp.transpose(pwi[t_len], (1, 0, 2)).reshape(1, ngrp * dp)
    return (kst_c, kout_c, xk_c, bt_c, dvec), ar, ai


def _state_to_cols(s_re, s_im, gb):
    b, _, ngrp, p_len = s_re.shape
    def blk(x):
        return jnp.transpose(x, (0, 2, 1, 3)).reshape(b, ngrp // gb, 1, gb * 2 * p_len)
    return jnp.concatenate([blk(s_re), blk(s_im)], axis=2).reshape(b, ngrp * 4 * p_len)


def _cols_to_state(cols, ngrp, p_len, gb):
    b = cols.shape[0]
    x = cols.reshape(b, ngrp // gb, 2, gb, 2, p_len)
    x = jnp.transpose(x, (2, 0, 4, 1, 3, 5)).reshape(2, b, 2, ngrp, p_len)
    return x[0], x[1]


def _to_chunks(u, nseq, seq_len):
    nchunk = seq_len // S5_CHUNK
    x = u.reshape(nseq, nchunk, S5_CHUNK, u.shape[-1])
    return jnp.transpose(x, (2, 1, 0, 3)).reshape(S5_CHUNK, nchunk * nseq, u.shape[-1]).astype(BF16)


def _from_chunks(y, nseq, seq_len):
    nchunk = seq_len // S5_CHUNK
    x = y.reshape(S5_CHUNK, nchunk, nseq, y.shape[-1])
    return jnp.transpose(x, (2, 1, 0, 3)).reshape(nseq * seq_len, y.shape[-1])


def _ctx_attn_kernel(q_ref, k_ref, v_ref, *rest, nh, dh):
    o_ref, kc_ref, vc_ref = rest[-3:]
    scale = dh ** -0.5
    for h in range(nh):
        sl = slice(h * dh, (h + 1) * dh)
        s = _dot_nt(q_ref[:, sl].astype(BF16), k_ref[:, sl].astype(BF16)) * scale
        e = jnp.exp(s - jnp.max(s, axis=-1, keepdims=True))
        o = _dot(e.astype(BF16), v_ref[:, sl].astype(BF16)) / jnp.sum(e, axis=-1, keepdims=True)
        o_ref[:, sl] = o.astype(o_ref.dtype)
    kc_ref[0, 0] = k_ref[...]
    vc_ref[0, 0] = v_ref[...]


def _ctx_attn(z, *, nbatch, seq, nh, dcols, e, n_even, caches=None):
    kern = functools.partial(_ctx_attn_kernel, nh=nh, dh=dcols // nh)
    in_specs = [pl.BlockSpec((seq, dcols), lambda b, c=c: (b, c)) for c in (1, 2, 3)]
    args = [z, z, z]
    aliases = {}
    if caches is not None:
        aliases = {len(args): 1, len(args) + 1: 2}
        in_specs += [pl.BlockSpec(memory_space=pl.ANY)] * 2
        args += list(caches)
    cache_spec = pl.BlockSpec((1, 1, seq, dcols), lambda b: (b, e, 0, 0))
    cache_shape = jax.ShapeDtypeStruct((nbatch, n_even, seq, dcols), F32)
    out = pl.pallas_call(
        kern,
        grid=(nbatch,),
        in_specs=in_specs,
        out_specs=[pl.BlockSpec((seq, dcols), lambda b: (b, 0)), cache_spec, cache_spec],
        out_shape=[jax.ShapeDtypeStruct((nbatch * seq, dcols), BF16), cache_shape, cache_shape],
        input_output_aliases=aliases,
        name="ctx_attn",
        compiler_params=_cparams(("arbitrary",)),
    )(*args)
    return out[0], (out[1], out[2])


def _na_window_start(i, rows, win_r):
    nkr = win_r + NA_QROWS
    first = jnp.clip(NA_QROWS * i - win_r // 2, 0, rows - win_r)
    return jnp.minimum(first, rows - nkr)


def _na_kernel(q_ref, kb_ref, vb_ref, kc_ref, vc_ref, bias_ref, o_ref, *, rows, win_r, dh, hb):
    i = pl.program_id(2)
    nk = (win_r + NA_QROWS) * GRID_W
    start = pl.multiple_of(_na_window_start(i, rows, win_r) * GRID_W, GRID_W)
    outs = []
    for h in range(hb):
        sl = slice(h * dh, (h + 1) * dh)
        q = (q_ref[:, sl] * (dh ** -0.5)).astype(BF16)
        s_loc = _dot_nt(q, kb_ref[pl.ds(start, nk), sl]) + bias_ref[0, h]
        s_ctx = _dot_nt(q, kc_ref[:, sl])
        m = jnp.maximum(jnp.max(s_loc, axis=-1, keepdims=True), jnp.max(s_ctx, axis=-1, keepdims=True))
        e_loc = jnp.exp(s_loc - m)
        e_ctx = jnp.exp(s_ctx - m)
        den = jnp.sum(e_loc, axis=-1, keepdims=True) + jnp.sum(e_ctx, axis=-1, keepdims=True)
        o = _dot(e_loc.astype(BF16), vb_ref[pl.ds(start, nk), sl]) + _dot(e_ctx.astype(BF16), vc_ref[:, sl])
        outs.append((o / den).astype(o_ref.dtype))
    o_ref[...] = jnp.concatenate(outs, axis=-1)


def _na_bias(rpb, rows):
    nh, nir, nic = rpb.shape
    win_r = (nir + 1) // 2
    win_c = (nic + 1) // 2
    wr = min(win_r, rows)
    nkr = wr + NA_QROWS
    nblk = rows // NA_QROWS
    col = np.arange(GRID_W)
    c0 = np.clip(col - win_c // 2, 0, GRID_W - win_c)
    col_ok = (col[None, :] >= c0[:, None]) & (col[None, :] < c0[:, None] + win_c)
    ic = np.clip(col[None, :] - col[:, None] + (win_c - 1), 0, 2 * win_c - 2)

    def row_pattern(i):
        start = min(int(np.clip(NA_QROWS * i - wr // 2, 0, rows - wr)), rows - nkr)
        r = NA_QROWS * i + np.arange(NA_QROWS)
        kr = start + np.arange(nkr)
        kstart = np.clip(r - wr // 2, 0, rows - wr)
        row_ok = (kr[None, :] >= kstart[:, None]) & (kr[None, :] < kstart[:, None] + wr)
        ir = np.clip(kr[None, :] - r[:, None] + (win_r - 1), 0, nir - 1)
        assert row_ok.sum(axis=1).min() == wr
        return row_ok, np.where(row_ok, ir, 0)

    for i in range(2, nblk - 1):
        assert all(np.array_equal(a, b) for a, b in zip(row_pattern(i), row_pattern(1)))

    sel_c = jnp.asarray((ic[None] == np.arange(nic)[:, None, None]).astype(np.float32))
    by_col = jnp.einsum('hic,cqk->hiqk', rpb.astype(F32), sel_c, precision=HI)
    tables = []
    for i in (0, 1, nblk - 1):
        row_ok, ir = row_pattern(i)
        ok = row_ok[:, None, :, None] & col_ok[None, :, None, :]
        sel_r = jnp.asarray((ir[None] == np.arange(nir)[:, None, None]).astype(np.float32))
        vals = jnp.einsum('hiqk,iaj->haqjk', by_col, sel_r, precision=HI)
        vals = jnp.where(jnp.asarray(ok)[None], vals, NEG_INF)
        tables.append(vals.reshape(nh, NA_QROWS * GRID_W, nkr * GRID_W))
    return jnp.stack(tables), wr


def _na_attn(z, kv, kc, vc, bias, *, nbatch, seq, past, nh, row0, win_r, dcols):
    dh = dcols // nh
    rows = seq // GRID_W
    tq = NA_QROWS * GRID_W
    nblk = rows // NA_QROWS
    assert row0 % seq == 0 and rows % NA_QROWS == 0 and rows >= win_r + NA_QROWS
    hb = _tile(nh, NA_HEADS)
    hw = hb * dh
    ng = nh // hb
    kern = functools.partial(_na_kernel, rows=rows, win_r=win_r, dh=dh, hb=hb)
    qoff = row0 // tq
    boff = row0 // seq

    def cls(i):
        return jnp.where(i == 0, 0, jnp.where(i == nblk - 1, 2, 1))

    c_spec = pl.BlockSpec((past, hw), lambda b, h, i: (b, h))
    return pl.pallas_call(
        kern,
        grid=(nbatch, ng, nblk),
        in_specs=[
            pl.BlockSpec((tq, hw), lambda b, h, i: (qoff + b * nblk + i, ng + h)),
            pl.BlockSpec((seq, hw), lambda b, h, i: (boff + b, h)),
            pl.BlockSpec((seq, hw), lambda b, h, i: (boff + b, ng + h)),
            c_spec, c_spec,
            pl.BlockSpec((1, hb) + bias.shape[2:], lambda b, h, i: (cls(i), h, 0, 0)),
        ],
        out_specs=pl.BlockSpec((tq, hw), lambda b, h, i: (b * nblk + i, h)),
        out_shape=jax.ShapeDtypeStruct((nbatch * seq, dcols), BF16),
        name="na_attn",
        compiler_params=_cparams(("arbitrary", "arbitrary", "arbitrary")),
    )(z, kv, kv, kc, vc, bias)


def _even_out_kernel(ysc_ref, ysl_ref, ybc_ref, ybl_ref, x_ref, m_ref, g_ref, wglu_ref, woa_ref, wob_ref,
                     o_ref, *, nc):
    is_ctx = pl.program_id(0) < nc
    ys = jnp.where(is_ctx, ysc_ref[...], ysl_ref[...])
    yb = jnp.where(is_ctx, ybc_ref[...], ybl_ref[...])
    y = jax.nn.gelu(ys.astype(F32))
    gate = jax.nn.sigmoid(_dot(y.astype(BF16), wglu_ref[...]))
    ya = (y * gate).astype(BF16)
    o = _dot(ya, woa_ref[...]) + _dot(yb, wob_ref[...])
    o_ref[...] = _post_residual(x_ref[...], o, g_ref[3:4, :], m_ref[0, 5:6, :])


def _even_out(ys_c, ys_l, yb_c, yb_l, x, mod_l, g_l, w_glu, w_out, *, gid, tm):
    n, d = x.shape
    da = ys_c.shape[1]
    db = yb_c.shape[1]
    nc = ys_c.shape[0] // tm
    assert da == db and nc * tm == ys_c.shape[0] == yb_c.shape[0]

    def ctx_blk(i):
        return (jnp.minimum(i, nc - 1), 0)

    def lat_blk(i):
        return (jnp.maximum(i - nc, 0), 0)

    return pl.pallas_call(
        functools.partial(_even_out_kernel, nc=nc),
        grid=(n // tm,),
        in_specs=[
            pl.BlockSpec((tm, da), ctx_blk),
            pl.BlockSpec((tm, da), lat_blk),
            pl.BlockSpec((tm, db), ctx_blk),
            pl.BlockSpec((tm, db), lat_blk),
            pl.BlockSpec((tm, d), lambda i: (i, 0)),
            pl.BlockSpec((1, N_MOD, d), lambda i: (gid(i), 0, 0)),
            pl.BlockSpec(g_l.shape, lambda i: (0, 0)),
            _resident((da, da), lambda i: (0, 0)),
            _resident((da, d), lambda i: (0, 0)),
            _resident((db, d), lambda i: (1, 0)),
        ],
        out_specs=pl.BlockSpec((tm, d), lambda i: (i, 0)),
        out_shape=jax.ShapeDtypeStruct((n, d), F32),
        name="even_out",
        compiler_params=_cparams(("arbitrary",)),
    )(ys_c, ys_l, yb_c, yb_l, x, mod_l, g_l, w_glu, w_out, w_out)


def _odd_in_kernel(x_ref, m_ref, g_ref, wb_ref, wc_ref, wx_ref, b_ref, p_ref, h_ref):
    def step(first):
        h = _mixer_in_h(first, x_ref, m_ref, g_ref, h_ref)
        b_ref[...] = _dot(h, wb_ref[...])
        p_ref[...] = _dot(h, wc_ref[...]) * _dot(h, wx_ref[...])

    j = pl.program_id(1)
    pl.when(j == 0)(functools.partial(step, True))
    pl.when(j > 0)(functools.partial(step, False))


def _odd_in(x, mod_l, g_l, w, *, gid, tm):
    n, d = x.shape
    dc = w.shape[1] // 3
    tn = _tile(dc, 512)
    nj = dc // tn
    blk = pl.BlockSpec((tm, tn), lambda i, j: (i, j))
    return pl.pallas_call(
        _odd_in_kernel,
        grid=(n // tm, nj),
        in_specs=[
            pl.BlockSpec((tm, d), lambda i, j: (i, 0)),
            pl.BlockSpec((1, N_MOD, d), lambda i, j: (gid(i), 0, 0)),
            pl.BlockSpec(g_l.shape, lambda i, j: (0, 0)),
            pl.BlockSpec((d, tn), lambda i, j: (0, j)),
            pl.BlockSpec((d, tn), lambda i, j: (0, j + nj)),
            pl.BlockSpec((d, tn), lambda i, j: (0, j + 2 * nj)),
        ],
        out_specs=[blk, blk],
        out_shape=[jax.ShapeDtypeStruct((n, dc), F32), jax.ShapeDtypeStruct((n, dc), F32)],
        scratch_shapes=[pltpu.VMEM((tm, d), BF16)],
        name="odd_in",
        compiler_params=_cparams(("arbitrary", "arbitrary")),
    )(x, mod_l, g_l, w, w, w)


def _odd_out_kernel(b_ref, p_ref, pp_ref, pn_ref, cw_ref, x_ref, m_ref, g_ref, w_ref, o_ref,
                    *, tm, ctx_rows, ctx_seq, lat_seq):
    i = pl.program_id(0)
    p = p_ref[...]
    row = lax.broadcasted_iota(jnp.int32, (tm, 1), 0)
    grow = i * tm + row
    is_ctx = grow < ctx_rows
    pos = jnp.where(is_ctx, grow % ctx_seq, (grow - ctx_rows) % lat_seq)
    last = jnp.where(is_ctx, ctx_seq - 1, lat_seq - 1)
    prev = pltpu.roll(p, 1, axis=0)
    prev = jnp.where(row == 0, pp_ref[SUBLANE - 1:SUBLANE, :], prev)
    prev = jnp.where(pos == 0, 0.0, prev)
    nxt = pltpu.roll(p, tm - 1, axis=0)
    nxt = jnp.where(row == tm - 1, pn_ref[0:1, :], nxt)
    nxt = jnp.where(pos == last, 0.0, nxt)
    y = cw_ref[0:1, :] * prev + cw_ref[1:2, :] * p + cw_ref[2:3, :] * nxt
    v = (b_ref[...] * y).astype(BF16)
    o = _dot(v, w_ref[...])
    o_ref[...] = _post_residual(x_ref[...], o, g_ref[3:4, :], m_ref[0, 5:6, :])


def _odd_out(b, p, conv_w, x, mod_l, g_l, w, *, gid, tm, ctx_rows, ctx_seq, lat_seq):
    n, d = x.shape
    dc = b.shape[1]
    assert conv_w.shape[0] == 3
    kern = functools.partial(_odd_out_kernel, tm=tm, ctx_rows=ctx_rows, ctx_seq=ctx_seq, lat_seq=lat_seq)
    tpb = tm // SUBLANE
    nb8 = n // SUBLANE
    return pl.pallas_call(
        kern,
        grid=(n // tm,),
        in_specs=[
            pl.BlockSpec((tm, dc), lambda i: (i, 0)),
            pl.BlockSpec((tm, dc), lambda i: (i, 0)),
            pl.BlockSpec((SUBLANE, dc), lambda i: (jnp.maximum(i * tpb - 1, 0), 0)),
            pl.BlockSpec((SUBLANE, dc), lambda i: (jnp.minimum((i + 1) * tpb, nb8 - 1), 0)),
            pl.BlockSpec(conv_w.shape, lambda i: (0, 0)),
            pl.BlockSpec((tm, d), lambda i: (i, 0)),
            pl.BlockSpec((1, N_MOD, d), lambda i: (gid(i), 0, 0)),
            pl.BlockSpec(g_l.shape, lambda i: (0, 0)),
            _resident((dc, d), lambda i: (0, 0)),
        ],
        out_specs=pl.BlockSpec((tm, d), lambda i: (i, 0)),
        out_shape=jax.ShapeDtypeStruct((n, d), F32),
        name="odd_out",
        compiler_params=_cparams(("arbitrary",)),
    )(b, p, p, p, conv_w, x, mod_l, g_l, w)


def kernel(x_prompt, x_sample, cache_k, cache_v, state_s5_re, state_s5_im, c, c_ctx, norm_g, w_mod, b_mod,
           w_ffn_in, w_ffn_out, w_in_even, w_out_even, s5_lam_re, s5_lam_im, s5_log_dt, s5_b_re, s5_b_im,
           s5_c_re, s5_c_im, s5_d, w_glu, na_rpb, w_in_conv, conv_w, w_out_conv):
    nb_c, seq_c, d = x_prompt.shape
    nb_l, seq_l, _ = x_sample.shape
    depth = norm_g.shape[0]
    past, nh = cache_k.shape[2], cache_k.shape[3]
    ngrp, p_len = state_s5_re.shape[3], state_s5_re.shape[4]
    h_len = s5_b_re.shape[-1]
    d_a = ngrp * h_len
    ctx_rows = nb_c * seq_c
    n = ctx_rows + nb_l * seq_l
    tm_w = _tile(math.gcd(ctx_rows, seq_l), 1024)
    tm = _tile(tm_w, 512)
    tm_c = _tile(tm, 256)
    gb = LANE // h_len
    assert gb * h_len == LANE and ngrp % gb == 0 and 2 * p_len == LANE and 2 * d_a == d

    def group_of_tile(rows_per_tile):
        def gid(i):
            r0 = i * rows_per_tile
            return jnp.where(r0 < ctx_rows, 0, 1 + jnp.maximum(r0 - ctx_rows, 0) // seq_l)
        return gid

    gid = group_of_tile(tm)
    gid_c = group_of_tile(tm_c)
    gid_w = group_of_tile(tm_w)

    ng = 1 + nb_l
    ng8 = -(-ng // SUBLANE) * SUBLANE
    cvec = jnp.concatenate([c_ctx[None], c, jnp.zeros((ng8 - ng, d), F32)], axis=0)
    mod = _modulation(cvec, w_mod, b_mod).reshape(depth, ng8, N_MOD, d)


    w_ffn_in_b = w_ffn_in.astype(BF16)
    w_ffn_out_b = w_ffn_out.astype(BF16)

    n_even = w_in_even.shape[0]
    caches = None
    new_sre, new_sim = [], []
    for l in range(depth):
        g_l = norm_g[l]
        mod_l = mod[l]
        ffn1 = functools.partial(_ffn, mod_l=mod_l, g_l=g_l, w_in=w_ffn_in_b, w_out=w_ffn_out_b, l=l, k=0,
                                 mi=0, gi=0, gid=gid, tm=tm)
        if l > 0:
            x = ffn1(x)
        else:
            x = ffn1(x_prompt.reshape(ctx_rows, d), out_rows=n)
            x = ffn1(x_sample.reshape(n - ctx_rows, d), stream_row0=ctx_rows, out_rows=n, out_row0=ctx_rows,
                     into=x)
        if l % 2 == 0:
            e = l // 2
            z, u, kv = _even_in(x, mod_l, g_l, w_in_even[e].astype(BF16), gid=gid_w, tm=tm_w, du=d_a)
            factors, ar, ai = _s5_weights(s5_lam_re[e], s5_lam_im[e], s5_log_dt[e], s5_b_re[e],
                                          s5_b_im[e], s5_c_re[e], s5_c_im[e], s5_d[e], gb)
            h0_c = jnp.zeros((nb_c, ngrp * 4 * p_len), F32)
            ys_c, hfin = _s5(_to_chunks(u[:ctx_rows], nb_c, seq_c), factors, ar, ai, h0_c, nseq=nb_c, gb=gb)
            fin_re, fin_im = _cols_to_state(hfin, ngrp, p_len, gb)
            new_sre.append(fin_re)
            new_sim.append(fin_im)
            h0_l = _state_to_cols(state_s5_re[:, e], state_s5_im[:, e], gb)
            ys_l, _ = _s5(_to_chunks(u[ctx_rows:], nb_l, seq_l), factors, ar, ai, h0_l, nseq=nb_l, gb=gb)
            yb_c, caches = _ctx_attn(z, nbatch=nb_c, seq=seq_c, nh=nh, dcols=d_a, e=e, n_even=n_even,
                                     caches=caches)
            bias, win_r = _na_bias(na_rpb[e], seq_l // GRID_W)
            yb_l = _na_attn(z, kv, cache_k[:, e].reshape(nb_l * past, -1).astype(BF16),
                            cache_v[:, e].reshape(nb_l * past, -1).astype(BF16), bias,
                            nbatch=nb_l, seq=seq_l, past=past, nh=nh, row0=ctx_rows, win_r=win_r, dcols=d_a)
            x = _even_out(_from_chunks(ys_c, nb_c, seq_c), _from_chunks(ys_l, nb_l, seq_l), yb_c, yb_l,
                          x, mod_l, g_l, w_glu[e].astype(BF16), w_out_even[e].astype(BF16), gid=gid, tm=tm)
        else:
            o = l // 2
            b, p = _odd_in(x, mod_l, g_l, w_in_conv[o].astype(BF16), gid=gid_w, tm=tm_w)
            x = _odd_out(b, p, conv_w[o], x, mod_l, g_l, w_out_conv[o].astype(BF16), gid=gid_c, tm=tm_c,
                         ctx_rows=ctx_rows, ctx_seq=seq_c, lat_seq=seq_l)
        ffn2 = functools.partial(_ffn, x, mod_l, g_l, w_ffn_in_b, w_ffn_out_b, l=l, k=1,
                                 mi=6, gi=4, gid=gid, tm=tm)
        if l < depth - 1:
            x = ffn2()
        else:
            y_prompt = ffn2(row0=0, nrows=ctx_rows).reshape(nb_c, seq_c, d)
            y_sample = ffn2(row0=ctx_rows, nrows=n - ctx_rows).reshape(nb_l, seq_l, d)

    new_k, new_v = (cch.reshape(nb_c, n_even, seq_c, nh, -1) for cch in caches)
    return (y_prompt, y_sample, new_k, new_v,
            jnp.stack(new_sre, axis=1), jnp.stack(new_sim, axis=1))
```

```python
import functools
import math

import numpy as np
import jax
import jax.numpy as jnp
from jax import lax
from jax.experimental import pallas as pl
from jax.experimental.pallas import tpu as pltpu

F32 = jnp.float32
BF16 = jnp.bfloat16

NORM_EPS = 1e-6
N_MOD = 9
GRID_W = 64
NEG_INF = -1e30
S5_CHUNK = 8
NA_QROWS = 4
NA_HEADS = 4
LANE = 128
SUBLANE = 8
VMEM_LIMIT_BYTES = 56 * 1024 * 1024
HI = lax.Precision.HIGHEST


def _cparams(sem):
    return pltpu.CompilerParams(dimension_semantics=sem, vmem_limit_bytes=VMEM_LIMIT_BYTES)


def _tile(n, pref):
    t = min(n, pref)
    while n % t:
        t //= 2
    return t


def _resident(block_shape, index_map):
    return pl.BlockSpec(block_shape, index_map, pipeline_mode=pl.Buffered(1))


def _rms(x):
    return x * lax.rsqrt(jnp.mean(x * x, axis=-1, keepdims=True) + NORM_EPS)


def _modnorm(x, g, shift, scale):
    return _rms(x) * (g * (1.0 + scale)) + shift


def _post_residual(x, o, g, gate):
    return x + _rms(o) * (gate * g)


def _dot(a, b):
    return jnp.dot(a, b, preferred_element_type=F32)


def _dot_nt(a, b):
    return lax.dot_general(a, b, (((1,), (1,)), ((), ())), preferred_element_type=F32)


def _mod_kernel(c_ref, w_ref, b_ref, o_ref):
    c = c_ref[...]
    s = (c * jax.nn.sigmoid(c)).astype(BF16)
    o_ref[0] = _dot(s, w_ref[0].astype(BF16)) + b_ref[0]


def _modulation(cvec, w_mod, b_mod):
    depth, d, nd = w_mod.shape
    ng = cvec.shape[0]
    tn = _tile(nd, 1024)
    return pl.pallas_call(
        _mod_kernel,
        grid=(depth, nd // tn),
        in_specs=[
            pl.BlockSpec((ng, d), lambda l, j: (0, 0)),
            pl.BlockSpec((1, d, tn), lambda l, j: (l, 0, j)),
            pl.BlockSpec((1, 1, tn), lambda l, j: (l, 0, j)),
        ],
        out_specs=pl.BlockSpec((1, ng, tn), lambda l, j: (l, 0, j)),
        out_shape=jax.ShapeDtypeStruct((depth, ng, nd), F32),
        name="modulation",
        compiler_params=_cparams(("arbitrary", "arbitrary")),
    )(cvec, w_mod, b_mod.reshape(depth, 1, nd))


def _ffn_kernel(x_ref, m_ref, g_ref, wg_ref, wu_ref, wo_ref, *rest, mi, gi, nj):
    o_ref, h_ref = rest[-2:]
    j = pl.program_id(1)

    def step(first, last):
        if first:
            h = _modnorm(x_ref[...], g_ref[gi:gi + 1, :], m_ref[0, mi:mi + 1, :],
                         m_ref[0, mi + 1:mi + 2, :]).astype(BF16)
            h_ref[...] = h
        else:
            h = h_ref[...]
        a = _dot(h, wg_ref[0, 0])
        u = _dot(h, wu_ref[0, 0])
        hid = (a * jax.nn.sigmoid(a) * u).astype(BF16)
        acc = _dot(hid, wo_ref[0, 0])
        if not first:
            acc = o_ref[...] + acc
        if last:
            acc = _post_residual(x_ref[...], acc, g_ref[gi + 1:gi + 2, :], 0.5 * m_ref[0, mi + 2:mi + 3, :])
        o_ref[...] = acc

    if nj == 1:
        step(True, True)
        return
    pl.when(j == 0)(functools.partial(step, True, False))
    if nj > 2:
        pl.when(jnp.logical_and(j > 0, j < nj - 1))(functools.partial(step, False, False))
    pl.when(j == nj - 1)(functools.partial(step, False, True))


def _ffn(x, mod_l, g_l, w_in, w_out, *, l, k, mi, gi, gid, tm, row0=0, nrows=None, stream_row0=None,
         out_rows=None, out_row0=0, into=None):
    d = x.shape[1]
    n = x.shape[0] if nrows is None else nrows
    t0 = row0 // tm
    s0 = t0 if stream_row0 is None else stream_row0 // tm
    o0 = out_row0 // tm
    dff = w_out.shape[2]
    tf = _tile(dff, 512)
    nj = dff // tf
    kern = functools.partial(_ffn_kernel, mi=mi, gi=gi, nj=nj)
    in_specs = [
        pl.BlockSpec((tm, d), lambda i, j: (t0 + i, 0)),
        pl.BlockSpec((1, N_MOD, d), lambda i, j: (gid(s0 + i), 0, 0)),
        pl.BlockSpec(g_l.shape, lambda i, j: (0, 0)),
        pl.BlockSpec((1, 1, d, tf), lambda i, j: (l, k, 0, j)),
        pl.BlockSpec((1, 1, d, tf), lambda i, j: (l, k, 0, j + nj)),
        pl.BlockSpec((1, 1, tf, d), lambda i, j: (l, k, j, 0)),
    ]
    args = [x, mod_l, g_l, w_in, w_in, w_out]
    aliases = {}
    if into is not None:
        aliases = {len(args): 0}
        in_specs.append(pl.BlockSpec(memory_space=pl.ANY))
        args.append(into)
    return pl.pallas_call(
        kern,
        grid=(n // tm, nj),
        in_specs=in_specs,
        out_specs=pl.BlockSpec((tm, d), lambda i, j: (o0 + i, 0)),
        out_shape=jax.ShapeDtypeStruct((n if out_rows is None else out_rows, d), F32),
        scratch_shapes=[pltpu.VMEM((tm, d), BF16)],
        input_output_aliases=aliases,
        name="ffn",
        compiler_params=_cparams(("arbitrary", "arbitrary")),
    )(*args)


def _mixer_in_h(first, x_ref, m_ref, g_ref, h_ref):
    if first:
        h = _modnorm(x_ref[...], g_ref[2:3, :], m_ref[0, 3:4, :], m_ref[0, 4:5, :]).astype(BF16)
        h_ref[...] = h
        return h
    return h_ref[...]


def _even_in_kernel(x_ref, m_ref, g_ref, w_ref, z_ref, u_ref, kv_ref, h_ref):
    def step(first, is_kv):
        z = _dot(_mixer_in_h(first, x_ref, m_ref, g_ref, h_ref), w_ref[...])
        z_ref[...] = z
        if first:
            u_ref[...] = z.astype(u_ref.dtype)
        if is_kv:
            kv_ref[...] = z.astype(kv_ref.dtype)

    j = pl.program_id(1)
    pl.when(j == 0)(functools.partial(step, True, False))
    pl.when(j == 1)(functools.partial(step, False, False))
    pl.when(j >= 2)(functools.partial(step, False, True))


def _even_in(x, mod_l, g_l, w, *, gid, tm, du):
    n, d = x.shape
    nz = w.shape[1]
    tn = du
    assert nz == 4 * du
    return pl.pallas_call(
        _even_in_kernel,
        grid=(n // tm, nz // tn),
        in_specs=[
            pl.BlockSpec((tm, d), lambda i, j: (i, 0)),
            pl.BlockSpec((1, N_MOD, d), lambda i, j: (gid(i), 0, 0)),
            pl.BlockSpec(g_l.shape, lambda i, j: (0, 0)),
            pl.BlockSpec((d, tn), lambda i, j: (0, j)),
        ],
        out_specs=[pl.BlockSpec((tm, tn), lambda i, j: (i, j)),
                   pl.BlockSpec((tm, du), lambda i, j: (i, 0)),
                   pl.BlockSpec((tm, du), lambda i, j: (i, jnp.maximum(j - 2, 0)))],
        out_shape=[jax.ShapeDtypeStruct((n, nz), F32), jax.ShapeDtypeStruct((n, du), BF16),
                   jax.ShapeDtypeStruct((n, 2 * du), BF16)],
        scratch_shapes=[pltpu.VMEM((tm, d), BF16)],
        name="even_in",
        compiler_params=_cparams(("arbitrary", "arbitrary")),
    )(x, mod_l, g_l, w)


def _split_bf16(v):
    hi = v.astype(BF16)
    return hi, (v - hi.astype(F32)).astype(BF16)


def _s5_expand(kstc_ref, koutc_ref, xk_ref, bt_ref, dvec_ref, kst_s, kfull_s, kout_s, *, gb, t_len):
    kdim, w = kst_s.shape
    w2 = w // 2
    dp = w2 // gb
    gh = kdim // t_len
    h_len = gh // gb
    grp_r = (lax.broadcasted_iota(jnp.int32, (kdim, 1), 0) // h_len) % gb
    grp_c = (lax.broadcasted_iota(jnp.int32, (1, kdim), 1) // h_len) % gb

    kc = kstc_ref[0]
    ko = koutc_ref[0].T
    for b in range(gb):
        for half in range(2):
            src = slice(half * dp, (half + 1) * dp)
            dst = slice(half * w2 + b * dp, half * w2 + (b + 1) * dp)
            kst_s[:, dst] = jnp.where(grp_r == b, kc[:, src], 0.0).astype(BF16)
            kout_s[dst, :] = jnp.where(grp_c == b, ko[src, :], 0.0).astype(BF16)

    lane = lax.broadcasted_iota(jnp.int32, (1, 2 * dp), 1)
    fwd_lane = (lane % dp) < (dp // 2)
    same_grp = grp_r[:gh] == grp_c[:, :gh]
    bt = bt_ref[0]
    xh, xl = _split_bf16(xk_ref[0])
    blocks = []
    for lanes in (fwd_lane, jnp.logical_not(fwd_lane)):
        bh, bl = _split_bf16(jnp.where(lanes, bt, 0.0))
        per_lag = []
        for k in range(t_len):
            rk = slice(k * gh, (k + 1) * gh)
            p = _dot_nt(bh, xh[rk]) + _dot_nt(bh, xl[rk]) + _dot_nt(bl, xh[rk])
            per_lag.append(jnp.where(same_grp, p, 0.0))
        blocks.append(per_lag)
    eye = lax.broadcasted_iota(jnp.int32, (gh, gh), 0) == lax.broadcasted_iota(jnp.int32, (gh, gh), 1)
    skip = jnp.where(eye, dvec_ref[0], 0.0)
    for s in range(t_len):
        for t in range(t_len):
            tile = skip if s == t else None
            if t >= s:
                tile = blocks[0][t - s] if tile is None else tile + blocks[0][t - s]
            if s >= t:
                tile = blocks[1][s - t] if tile is None else tile + blocks[1][s - t]
            kfull_s[s * gh:(s + 1) * gh, t * gh:(t + 1) * gh] = tile.astype(BF16)


def _s5_kernel(u_ref, kstc_ref, koutc_ref, xk_ref, bt_ref, dvec_ref, ar_ref, ai_ref, h0_ref, y_ref, hfin_ref,
               s_ref, kst_s, kfull_s, kout_s, *, nseq, nchunk, rt, gb):
    t_len, rows, _ = u_ref.shape
    w = s_ref.shape[2]
    w2 = w // 2
    _s5_expand(kstc_ref, koutc_ref, xk_ref, bt_ref, dvec_ref, kst_s, kfull_s, kout_s, gb=gb, t_len=t_len)

    def chunk_lhs(r0):
        return jnp.concatenate([u_ref[t, r0:r0 + rt, :] for t in range(t_len)], axis=-1)

    for r0 in range(0, rows, rt):
        res = _dot(chunk_lhs(r0), kst_s[...])
        s_ref[r0 // SUBLANE:(r0 + rt) // SUBLANE] = res.reshape(rt // SUBLANE, SUBLANE, w)

    ar = ar_ref[...]
    ai = ai_ref[...]

    def step(hr, hi, sr, si):
        return ar * hr - ai * hi + sr, ar * hi + ai * hr + si

    if nseq % SUBLANE == 0:
        nb = nseq // SUBLANE
        d0 = (lax.broadcasted_iota(jnp.int32, (nb, SUBLANE, w2), 2) & (LANE // 2)) == 0
        hr0 = h0_ref[:, 0:w2].reshape(nb, SUBLANE, w2)
        hi0 = h0_ref[:, w2:w].reshape(nb, SUBLANE, w2)

        def body(k, carry):
            hr, hi = carry
            k1 = nchunk - 1 - k
            b0r = s_ref[pl.ds(k * nb, nb), :, 0:w2]
            b0i = s_ref[pl.ds(k * nb, nb), :, w2:w]
            b1r = s_ref[pl.ds(k1 * nb, nb), :, 0:w2]
            b1i = s_ref[pl.ds(k1 * nb, nb), :, w2:w]
            s_ref[pl.ds(k * nb, nb), :, 0:w2] = jnp.where(d0, hr, b0r)
            s_ref[pl.ds(k * nb, nb), :, w2:w] = jnp.where(d0, hi, b0i)
            s_ref[pl.ds(k1 * nb, nb), :, 0:w2] = jnp.where(d0, b1r, hr)
            s_ref[pl.ds(k1 * nb, nb), :, w2:w] = jnp.where(d0, b1i, hi)
            return step(hr, hi, jnp.where(d0, b0r, b1r), jnp.where(d0, b0i, b1i))

        assert nchunk % 2 == 0
        hr, hi = lax.fori_loop(0, nchunk, body, (hr0, hi0))
        hfin_ref[:, 0:w2] = hr.reshape(nseq, w2)
        hfin_ref[:, w2:w] = hi.reshape(nseq, w2)
    else:
        half = SUBLANE // 2
        assert nseq == half and nchunk % 4 == 0
        nblk = nchunk // 2
        d0 = (lax.broadcasted_iota(jnp.int32, (half, w2), 1) & (LANE // 2)) == 0
        lo = slice(0, half)
        up = slice(half, SUBLANE)

        def body(k, carry):
            hr, hi = carry
            k1 = nblk - 1 - k
            b0r = s_ref[k, :, 0:w2]
            b0i = s_ref[k, :, w2:w]
            b1r = s_ref[k1, :, 0:w2]
            b1i = s_ref[k1, :, w2:w]
            for fa, ba in ((lo, up), (up, lo)):
                s_ref[k, fa, 0:w2] = jnp.where(d0, hr, b0r[fa])
                s_ref[k, fa, w2:w] = jnp.where(d0, hi, b0i[fa])
                s_ref[k1, ba, 0:w2] = jnp.where(d0, b1r[ba], hr)
                s_ref[k1, ba, w2:w] = jnp.where(d0, b1i[ba], hi)
                hr, hi = step(hr, hi, jnp.where(d0, b0r[fa], b1r[ba]), jnp.where(d0, b0i[fa], b1i[ba]))
            return hr, hi

        hr, hi = lax.fori_loop(0, nblk, body, (h0_ref[:, 0:w2], h0_ref[:, w2:w]))
        hfin_ref[:, 0:w2] = hr
        hfin_ref[:, w2:w] = hi

    for r0 in range(0, rows, rt):
        hp = s_ref[r0 // SUBLANE:(r0 + rt) // SUBLANE].reshape(rt, w).astype(BF16)
        y = _dot(chunk_lhs(r0), kfull_s[...]) + _dot(hp, kout_s[...])
        for t in range(t_len):
            y_ref[t, r0:r0 + rt, :] = y[:, t * LANE:(t + 1) * LANE].astype(y_ref.dtype)


def _s5(u_t, factors, ar, ai, h0, *, nseq, gb):
    t_len, rows, _ = u_t.shape
    nblk, kdim, dp2 = factors[0].shape
    sdim = gb * dp2
    nchunk = rows // nseq
    kern = functools.partial(_s5_kernel, nseq=nseq, nchunk=nchunk, rt=_tile(rows, 512), gb=gb)
    state_bytes = rows * sdim * 4
    wspec = pl.BlockSpec if 4 * state_bytes <= VMEM_LIMIT_BYTES else _resident
    return pl.pallas_call(
        kern,
        grid=(nblk,),
        in_specs=[
            pl.BlockSpec((t_len, rows, LANE), lambda i: (0, 0, i)),
        ] + [wspec((1,) + f.shape[1:], lambda i: (i, 0, 0)) for f in factors] + [
            pl.BlockSpec((1, sdim // 2), lambda i: (0, i)),
            pl.BlockSpec((1, sdim // 2), lambda i: (0, i)),
            pl.BlockSpec((nseq, sdim), lambda i: (0, i)),
        ],
        out_specs=[pl.BlockSpec((t_len, rows, LANE), lambda i: (0, 0, i)),
                   pl.BlockSpec((nseq, sdim), lambda i: (0, i))],
        out_shape=[jax.ShapeDtypeStruct(u_t.shape, BF16),
                   jax.ShapeDtypeStruct((nseq, nblk * sdim), F32)],
        scratch_shapes=[pltpu.VMEM((rows // SUBLANE, SUBLANE, sdim), F32),
                        pltpu.VMEM((kdim, sdim), BF16), pltpu.VMEM((kdim, kdim), BF16),
                        pltpu.VMEM((sdim, kdim), BF16)],
        name="s5_seq%d" % nseq,
        compiler_params=_cparams(("arbitrary",)),
    )(u_t, *factors, ar, ai, h0)


def _s5_weights(lam_re, lam_im, log_dt, b_re, b_im, c_re, c_im, d_skip, gb):
    t_len = S5_CHUNK
    _, ngrp, p_len = lam_re.shape
    h_len = b_re.shape[-1]
    dt = jnp.exp(log_dt)[..., None]
    e = jnp.exp(lam_re * dt)
    lbr = e * jnp.cos(lam_im * dt)
    lbi = e * jnp.sin(lam_im * dt)
    den = lam_re * lam_re + lam_im * lam_im
    fr = ((lbr - 1.0) * lam_re + lbi * lam_im) / den
    fi = (lbi * lam_re - (lbr - 1.0) * lam_im) / den
    br = fr[..., None] * b_re - fi[..., None] * b_im
    bi = fr[..., None] * b_im + fi[..., None] * b_re
    pr = [jnp.ones_like(lbr)]
    pi = [jnp.zeros_like(lbr)]
    for _ in range(t_len):
        pr.append(pr[-1] * lbr - pi[-1] * lbi)
        pi.append(pr[-2] * lbi + pi[-1] * lbr)
    pwr = jnp.stack(pr)
    pwi = jnp.stack(pi)

    nblk = ngrp // gb
    dp = 2 * p_len
    tt = np.arange(t_len)

    def rows_of(tab):
        tab = jnp.transpose(tab, (0, 2, 1, 3)).reshape(t_len, nblk, gb, 1, dp)
        return jnp.transpose(tab, (1, 0, 2, 3, 4))

    def chan_of(x, perm):
        return jnp.transpose(x, perm).reshape(nblk, 1, gb, h_len, dp)

    btr, bti = chan_of(br, (1, 3, 0, 2)), chan_of(bi, (1, 3, 0, 2))
    ctr, cti = chan_of(c_re, (1, 2, 0, 3)), chan_of(c_im, (1, 2, 0, 3))

    def compact(tab_r, tab_i, ch_r, ch_i, im_sign):
        tr, ti = rows_of(tab_r), rows_of(tab_i)
        re = tr * ch_r - ti * ch_i
        im = tr * ch_i + ti * ch_r
        return jnp.concatenate([re, im_sign * im], axis=-1).reshape(nblk, t_len * gb * h_len, 2 * dp)

    def per_dir(fwd, bwd):
        return (jnp.stack([pwr[fwd, 0], pwr[bwd, 1]], axis=1), jnp.stack([pwi[fwd, 0], pwi[bwd, 1]], axis=1))

    kst_c = compact(*per_dir(t_len - 1 - tt, tt), btr, bti, 1.0)
    kout_c = compact(*per_dir(tt + 1, t_len - tt), ctr, cti, -1.0)
    xk_c = compact(*per_dir(tt, tt), ctr, cti, 1.0)
    bt_c = jnp.concatenate([btr, -bti], axis=-1).reshape(nblk, gb * h_len, 2 * dp)
    dvec = d_skip.reshape(nblk, 1, gb * h_len)

    ar = jnp.transpose(pwr[t_len], (1, 0, 2)).reshape(1, ngrp * dp)
    ai = jnp.transpose(pwi[t_len], (1, 0, 2)).reshape(1, ngrp * dp)
    return (kst_c, kout_c, xk_c, bt_c, dvec), ar, ai


def _state_to_cols(s_re, s_im, gb):
    b, _, ngrp, p_len = s_re.shape
    def blk(x):
        return jnp.transpose(x, (0, 2, 1, 3)).reshape(b, ngrp // gb, 1, gb * 2 * p_len)
    return jnp.concatenate([blk(s_re), blk(s_im)], axis=2).reshape(b, ngrp * 4 * p_len)


def _cols_to_state(cols, ngrp, p_len, gb):
    b = cols.shape[0]
    x = cols.reshape(b, ngrp // gb, 2, gb, 2, p_len)
    x = jnp.transpose(x, (2, 0, 4, 1, 3, 5)).reshape(2, b, 2, ngrp, p_len)
    return x[0], x[1]


def _to_chunks(u, nseq, seq_len):
    nchunk = seq_len // S5_CHUNK
    x = u.reshape(nseq, nchunk, S5_CHUNK, u.shape[-1])
    return jnp.transpose(x, (2, 1, 0, 3)).reshape(S5_CHUNK, nchunk * nseq, u.shape[-1]).astype(BF16)


def _from_chunks(y, nseq, seq_len):
    nchunk = seq_len // S5_CHUNK
    x = y.reshape(S5_CHUNK, nchunk, nseq, y.shape[-1])
    return jnp.transpose(x, (2, 1, 0, 3)).reshape(nseq * seq_len, y.shape[-1])


def _ctx_attn_kernel(q_ref, k_ref, v_ref, *rest, nh, dh):
    o_ref, kc_ref, vc_ref = rest[-3:]
    scale = dh ** -0.5
    for h in range(nh):
        sl = slice(h * dh, (h + 1) * dh)
        s = _dot_nt(q_ref[:, sl].astype(BF16), k_ref[:, sl].astype(BF16)) * scale
        e = jnp.exp(s - jnp.max(s, axis=-1, keepdims=True))
        o = _dot(e.astype(BF16), v_ref[:, sl].astype(BF16)) / jnp.sum(e, axis=-1, keepdims=True)
        o_ref[:, sl] = o.astype(o_ref.dtype)
    kc_ref[0, 0] = k_ref[...]
    vc_ref[0, 0] = v_ref[...]


def _ctx_attn(z, *, nbatch, seq, nh, dcols, e, n_even, caches=None):
    kern = functools.partial(_ctx_attn_kernel, nh=nh, dh=dcols // nh)
    in_specs = [pl.BlockSpec((seq, dcols), lambda b, c=c: (b, c)) for c in (1, 2, 3)]
    args = [z, z, z]
    aliases = {}
    if caches is not None:
        aliases = {len(args): 1, len(args) + 1: 2}
        in_specs += [pl.BlockSpec(memory_space=pl.ANY)] * 2
        args += list(caches)
    cache_spec = pl.BlockSpec((1, 1, seq, dcols), lambda b: (b, e, 0, 0))
    cache_shape = jax.ShapeDtypeStruct((nbatch, n_even, seq, dcols), F32)
    out = pl.pallas_call(
        kern,
        grid=(nbatch,),
        in_specs=in_specs,
        out_specs=[pl.BlockSpec((seq, dcols), lambda b: (b, 0)), cache_spec, cache_spec],
        out_shape=[jax.ShapeDtypeStruct((nbatch * seq, dcols), BF16), cache_shape, cache_shape],
        input_output_aliases=aliases,
        name="ctx_attn",
        compiler_params=_cparams(("arbitrary",)),
    )(*args)
    return out[0], (out[1], out[2])


def _na_window_start(i, rows, win_r):
    nkr = win_r + NA_QROWS
    first = jnp.clip(NA_QROWS * i - win_r // 2, 0, rows - win_r)
    return jnp.minimum(first, rows - nkr)


def _na_kernel(q_ref, kb_ref, vb_ref, kc_ref, vc_ref, bias_ref, o_ref, *, rows, win_r, dh, hb):
    i = pl.program_id(2)
    nk = (win_r + NA_QROWS) * GRID_W
    start = pl.multiple_of(_na_window_start(i, rows, win_r) * GRID_W, GRID_W)
    outs = []
    for h in range(hb):
        sl = slice(h * dh, (h + 1) * dh)
        q = (q_ref[:, sl] * (dh ** -0.5)).astype(BF16)
        s_loc = _dot_nt(q, kb_ref[pl.ds(start, nk), sl]) + bias_ref[0, h]
        s_ctx = _dot_nt(q, kc_ref[0, 0, :, sl])
        m = jnp.maximum(jnp.max(s_loc, axis=-1, keepdims=True), jnp.max(s_ctx, axis=-1, keepdims=True))
        e_loc = jnp.exp(s_loc - m)
        e_ctx = jnp.exp(s_ctx - m)
        den = jnp.sum(e_loc, axis=-1, keepdims=True) + jnp.sum(e_ctx, axis=-1, keepdims=True)
        o = (_dot(e_loc.astype(BF16), vb_ref[pl.ds(start, nk), sl])
             + _dot(e_ctx.astype(BF16), vc_ref[0, 0, :, sl]))
        outs.append((o / den).astype(o_ref.dtype))
    o_ref[...] = jnp.concatenate(outs, axis=-1)


def _na_bias(rpb, rows):
    nh, nir, nic = rpb.shape
    win_r = (nir + 1) // 2
    win_c = (nic + 1) // 2
    wr = min(win_r, rows)
    nkr = wr + NA_QROWS
    nblk = rows // NA_QROWS
    col = np.arange(GRID_W)
    c0 = np.clip(col - win_c // 2, 0, GRID_W - win_c)
    col_ok = (col[None, :] >= c0[:, None]) & (col[None, :] < c0[:, None] + win_c)
    ic = np.clip(col[None, :] - col[:, None] + (win_c - 1), 0, 2 * win_c - 2)

    def row_pattern(i):
        start = min(int(np.clip(NA_QROWS * i - wr // 2, 0, rows - wr)), rows - nkr)
        r = NA_QROWS * i + np.arange(NA_QROWS)
        kr = start + np.arange(nkr)
        kstart = np.clip(r - wr // 2, 0, rows - wr)
        row_ok = (kr[None, :] >= kstart[:, None]) & (kr[None, :] < kstart[:, None] + wr)
        ir = np.clip(kr[None, :] - r[:, None] + (win_r - 1), 0, nir - 1)
        assert row_ok.sum(axis=1).min() == wr
        return row_ok, np.where(row_ok, ir, 0)

    for i in range(2, nblk - 1):
        assert all(np.array_equal(a, b) for a, b in zip(row_pattern(i), row_pattern(1)))

    sel_c = jnp.asarray((ic[None] == np.arange(nic)[:, None, None]).astype(np.float32))
    by_col = jnp.einsum('hic,cqk->hiqk', rpb.astype(F32), sel_c, precision=HI)
    tables = []
    for i in (0, 1, nblk - 1):
        row_ok, ir = row_pattern(i)
        ok = row_ok[:, None, :, None] & col_ok[None, :, None, :]
        sel_r = jnp.asarray((ir[None] == np.arange(nir)[:, None, None]).astype(np.float32))
        vals = jnp.einsum('hiqk,iaj->haqjk', by_col, sel_r, precision=HI)
        vals = jnp.where(jnp.asarray(ok)[None], vals, NEG_INF)
        tables.append(vals.reshape(nh, NA_QROWS * GRID_W, nkr * GRID_W))
    return jnp.stack(tables), wr


def _na_attn(z, kv, kc, vc, bias, *, e, nbatch, seq, past, nh, row0, win_r, dcols):
    dh = dcols // nh
    rows = seq // GRID_W
    tq = NA_QROWS * GRID_W
    nblk = rows // NA_QROWS
    assert row0 % seq == 0 and rows % NA_QROWS == 0 and rows >= win_r + NA_QROWS
    hb = _tile(nh, NA_HEADS)
    hw = hb * dh
    ng = nh // hb
    kern = functools.partial(_na_kernel, rows=rows, win_r=win_r, dh=dh, hb=hb)
    qoff = row0 // tq
    boff = row0 // seq

    def cls(i):
        return jnp.where(i == 0, 0, jnp.where(i == nblk - 1, 2, 1))

    c_spec = pl.BlockSpec((1, 1, past, hw), lambda b, h, i: (b, e, 0, h))
    return pl.pallas_call(
        kern,
        grid=(nbatch, ng, nblk),
        in_specs=[
            pl.BlockSpec((tq, hw), lambda b, h, i: (qoff + b * nblk + i, ng + h)),
            pl.BlockSpec((seq, hw), lambda b, h, i: (boff + b, h)),
            pl.BlockSpec((seq, hw), lambda b, h, i: (boff + b, ng + h)),
            c_spec, c_spec,
            pl.BlockSpec((1, hb) + bias.shape[2:], lambda b, h, i: (cls(i), e * ng + h, 0, 0)),
        ],
        out_specs=pl.BlockSpec((tq, hw), lambda b, h, i: (b * nblk + i, h)),
        out_shape=jax.ShapeDtypeStruct((nbatch * seq, dcols), BF16),
        name="na_attn",
        compiler_params=_cparams(("arbitrary", "arbitrary", "arbitrary")),
    )(z, kv, kv, kc, vc, bias)


def _even_out_kernel(ysc_ref, ysl_ref, ybc_ref, ybl_ref, x_ref, m_ref, g_ref, wglu_ref, woa_ref, wob_ref,
                     o_ref, *, nc):
    is_ctx = pl.program_id(0) < nc
    ys = jnp.where(is_ctx, ysc_ref[...], ysl_ref[...])
    yb = jnp.where(is_ctx, ybc_ref[...], ybl_ref[...])
    y = jax.nn.gelu(ys.astype(F32))
    gate = jax.nn.sigmoid(_dot(y.astype(BF16), wglu_ref[...]))
    ya = (y * gate).astype(BF16)
    o = _dot(ya, woa_ref[...]) + _dot(yb, wob_ref[...])
    o_ref[...] = _post_residual(x_ref[...], o, g_ref[3:4, :], m_ref[0, 5:6, :])


def _even_out(ys_c, ys_l, yb_c, yb_l, x, mod_l, g_l, w_glu, w_out, *, gid, tm):
    n, d = x.shape
    da = ys_c.shape[1]
    db = yb_c.shape[1]
    nc = ys_c.shape[0] // tm
    assert da == db and nc * tm == ys_c.shape[0] == yb_c.shape[0]

    def ctx_blk(i):
        return (jnp.minimum(i, nc - 1), 0)

    def lat_blk(i):
        return (jnp.maximum(i - nc, 0), 0)

    return pl.pallas_call(
        functools.partial(_even_out_kernel, nc=nc),
        grid=(n // tm,),
        in_specs=[
            pl.BlockSpec((tm, da), ctx_blk),
            pl.BlockSpec((tm, da), lat_blk),
            pl.BlockSpec((tm, db), ctx_blk),
            pl.BlockSpec((tm, db), lat_blk),
            pl.BlockSpec((tm, d), lambda i: (i, 0)),
            pl.BlockSpec((1, N_MOD, d), lambda i: (gid(i), 0, 0)),
            pl.BlockSpec(g_l.shape, lambda i: (0, 0)),
            _resident((da, da), lambda i: (0, 0)),
            _resident((da, d), lambda i: (0, 0)),
            _resident((db, d), lambda i: (1, 0)),
        ],
        out_specs=pl.BlockSpec((tm, d), lambda i: (i, 0)),
        out_shape=jax.ShapeDtypeStruct((n, d), F32),
        name="even_out",
        compiler_params=_cparams(("arbitrary",)),
    )(ys_c, ys_l, yb_c, yb_l, x, mod_l, g_l, w_glu, w_out, w_out)


def _odd_in_kernel(x_ref, m_ref, g_ref, wb_ref, wc_ref, wx_ref, b_ref, p_ref, h_ref):
    def step(first):
        h = _mixer_in_h(first, x_ref, m_ref, g_ref, h_ref)
        b_ref[...] = _dot(h, wb_ref[...])
        p_ref[...] = _dot(h, wc_ref[...]) * _dot(h, wx_ref[...])

    j = pl.program_id(1)
    pl.when(j == 0)(functools.partial(step, True))
    pl.when(j > 0)(functools.partial(step, False))


def _odd_in(x, mod_l, g_l, w, *, gid, tm):
    n, d = x.shape
    dc = w.shape[1] // 3
    tn = _tile(dc, 512)
    nj = dc // tn
    blk = pl.BlockSpec((tm, tn), lambda i, j: (i, j))
    return pl.pallas_call(
        _odd_in_kernel,
        grid=(n // tm, nj),
        in_specs=[
            pl.BlockSpec((tm, d), lambda i, j: (i, 0)),
            pl.BlockSpec((1, N_MOD, d), lambda i, j: (gid(i), 0, 0)),
            pl.BlockSpec(g_l.shape, lambda i, j: (0, 0)),
            pl.BlockSpec((d, tn), lambda i, j: (0, j)),
            pl.BlockSpec((d, tn), lambda i, j: (0, j + nj)),
            pl.BlockSpec((d, tn), lambda i, j: (0, j + 2 * nj)),
        ],
        out_specs=[blk, blk],
        out_shape=[jax.ShapeDtypeStruct((n, dc), F32), jax.ShapeDtypeStruct((n, dc), F32)],
        scratch_shapes=[pltpu.VMEM((tm, d), BF16)],
        name="odd_in",
        compiler_params=_cparams(("arbitrary", "arbitrary")),
    )(x, mod_l, g_l, w, w, w)


def _odd_out_kernel(b_ref, p_ref, pp_ref, pn_ref, cw_ref, x_ref, m_ref, g_ref, w_ref, o_ref,
                    *, tm, ctx_rows, ctx_seq, lat_seq):
    i = pl.program_id(0)
    p = p_ref[...]
    row = lax.broadcasted_iota(jnp.int32, (tm, 1), 0)
    grow = i * tm + row
    is_ctx = grow < ctx_rows
    pos = jnp.where(is_ctx, grow % ctx_seq, (grow - ctx_rows) % lat_seq)
    last = jnp.where(is_ctx, ctx_seq - 1, lat_seq - 1)
    prev = pltpu.roll(p, 1, axis=0)
    prev = jnp.where(row == 0, pp_ref[SUBLANE - 1:SUBLANE, :], prev)
    prev = jnp.where(pos == 0, 0.0, prev)
    nxt = pltpu.roll(p, tm - 1, axis=0)
    nxt = jnp.where(row == tm - 1, pn_ref[0:1, :], nxt)
    nxt = jnp.where(pos == last, 0.0, nxt)
    y = cw_ref[0:1, :] * prev + cw_ref[1:2, :] * p + cw_ref[2:3, :] * nxt
    v = (b_ref[...] * y).astype(BF16)
    o = _dot(v, w_ref[...])
    o_ref[...] = _post_residual(x_ref[...], o, g_ref[3:4, :], m_ref[0, 5:6, :])


def _odd_out(b, p, conv_w, x, mod_l, g_l, w, *, gid, tm, ctx_rows, ctx_seq, lat_seq):
    n, d = x.shape
    dc = b.shape[1]
    assert conv_w.shape[0] == 3
    kern = functools.partial(_odd_out_kernel, tm=tm, ctx_rows=ctx_rows, ctx_seq=ctx_seq, lat_seq=lat_seq)
    tpb = tm // SUBLANE
    nb8 = n // SUBLANE
    return pl.pallas_call(
        kern,
        grid=(n // tm,),
        in_specs=[
            pl.BlockSpec((tm, dc), lambda i: (i, 0)),
            pl.BlockSpec((tm, dc), lambda i: (i, 0)),
            pl.BlockSpec((SUBLANE, dc), lambda i: (jnp.maximum(i * tpb - 1, 0), 0)),
            pl.BlockSpec((SUBLANE, dc), lambda i: (jnp.minimum((i + 1) * tpb, nb8 - 1), 0)),
            pl.BlockSpec(conv_w.shape, lambda i: (0, 0)),
            pl.BlockSpec((tm, d), lambda i: (i, 0)),
            pl.BlockSpec((1, N_MOD, d), lambda i: (gid(i), 0, 0)),
            pl.BlockSpec(g_l.shape, lambda i: (0, 0)),
            _resident((dc, d), lambda i: (0, 0)),
        ],
        out_specs=pl.BlockSpec((tm, d), lambda i: (i, 0)),
        out_shape=jax.ShapeDtypeStruct((n, d), F32),
        name="odd_out",
        compiler_params=_cparams(("arbitrary",)),
    )(b, p, p, p, conv_w, x, mod_l, g_l, w)


def kernel(x_prompt, x_sample, cache_k, cache_v, state_s5_re, state_s5_im, c, c_ctx, norm_g, w_mod, b_mod,
           w_ffn_in, w_ffn_out, w_in_even, w_out_even, s5_lam_re, s5_lam_im, s5_log_dt, s5_b_re, s5_b_im,
           s5_c_re, s5_c_im, s5_d, w_glu, na_rpb, w_in_conv, conv_w, w_out_conv):
    nb_c, seq_c, d = x_prompt.shape
    nb_l, seq_l, _ = x_sample.shape
    depth = norm_g.shape[0]
    past, nh = cache_k.shape[2], cache_k.shape[3]
    ngrp, p_len = state_s5_re.shape[3], state_s5_re.shape[4]
    h_len = s5_b_re.shape[-1]
    d_a = ngrp * h_len
    ctx_rows = nb_c * seq_c
    n = ctx_rows + nb_l * seq_l
    tm_w = _tile(math.gcd(ctx_rows, seq_l), 1024)
    tm = _tile(tm_w, 512)
    tm_c = _tile(tm, 256)
    gb = LANE // h_len
    assert gb * h_len == LANE and ngrp % gb == 0 and 2 * p_len == LANE and 2 * d_a == d

    def group_of_tile(rows_per_tile):
        def gid(i):
            r0 = i * rows_per_tile
            return jnp.where(r0 < ctx_rows, 0, 1 + jnp.maximum(r0 - ctx_rows, 0) // seq_l)
        return gid

    gid = group_of_tile(tm)
    gid_c = group_of_tile(tm_c)
    gid_w = group_of_tile(tm_w)

    ng = 1 + nb_l
    ng8 = -(-ng // SUBLANE) * SUBLANE
    cvec = jnp.concatenate([c_ctx[None], c, jnp.zeros((ng8 - ng, d), F32)], axis=0)
    mod = _modulation(cvec, w_mod, b_mod).reshape(depth, ng8, N_MOD, d)


    cache_k_b = cache_k.reshape(cache_k.shape[:3] + (-1,)).astype(BF16)
    cache_v_b = cache_v.reshape(cache_v.shape[:3] + (-1,)).astype(BF16)
    bias, win_r = _na_bias(na_rpb.reshape((-1,) + na_rpb.shape[2:]), seq_l // GRID_W)
    w_ffn_in_b = w_ffn_in.astype(BF16)
    w_ffn_out_b = w_ffn_out.astype(BF16)

    n_even = w_in_even.shape[0]
    caches = None
    new_sre, new_sim = [], []
    for l in range(depth):
        g_l = norm_g[l]
        mod_l = mod[l]
        ffn1 = functools.partial(_ffn, mod_l=mod_l, g_l=g_l, w_in=w_ffn_in_b, w_out=w_ffn_out_b, l=l, k=0,
                                 mi=0, gi=0, gid=gid, tm=tm)
        if l > 0:
            x = ffn1(x)
        else:
            x = ffn1(x_prompt.reshape(ctx_rows, d), out_rows=n)
            x = ffn1(x_sample.reshape(n - ctx_rows, d), stream_row0=ctx_rows, out_rows=n, out_row0=ctx_rows,
                     into=x)
        if l % 2 == 0:
            e = l // 2
            z, u, kv = _even_in(x, mod_l, g_l, w_in_even[e].astype(BF16), gid=gid_w, tm=tm_w, du=d_a)
            factors, ar, ai = _s5_weights(s5_lam_re[e], s5_lam_im[e], s5_log_dt[e], s5_b_re[e],
                                          s5_b_im[e], s5_c_re[e], s5_c_im[e], s5_d[e], gb)
            h0_c = jnp.zeros((nb_c, ngrp * 4 * p_len), F32)
            ys_c, hfin = _s5(_to_chunks(u[:ctx_rows], nb_c, seq_c), factors, ar, ai, h0_c, nseq=nb_c, gb=gb)
            fin_re, fin_im = _cols_to_state(hfin, ngrp, p_len, gb)
            new_sre.append(fin_re)
            new_sim.append(fin_im)
            h0_l = _state_to_cols(state_s5_re[:, e], state_s5_im[:, e], gb)
            ys_l, _ = _s5(_to_chunks(u[ctx_rows:], nb_l, seq_l), factors, ar, ai, h0_l, nseq=nb_l, gb=gb)
            yb_c, caches = _ctx_attn(z, nbatch=nb_c, seq=seq_c, nh=nh, dcols=d_a, e=e, n_even=n_even,
                                     caches=caches)
            yb_l = _na_attn(z, kv, cache_k_b, cache_v_b, bias, e=e,
                            nbatch=nb_l, seq=seq_l, past=past, nh=nh, row0=ctx_rows, win_r=win_r, dcols=d_a)
            x = _even_out(_from_chunks(ys_c, nb_c, seq_c), _from_chunks(ys_l, nb_l, seq_l), yb_c, yb_l,
                          x, mod_l, g_l, w_glu[e].astype(BF16), w_out_even[e].astype(BF16), gid=gid, tm=tm)
        else:
            o = l // 2
            b, p = _odd_in(x, mod_l, g_l, w_in_conv[o].astype(BF16), gid=gid_w, tm=tm_w)
            x = _odd_out(b, p, conv_w[o], x, mod_l, g_l, w_out_conv[o].astype(BF16), gid=gid_c, tm=tm_c,
                         ctx_rows=ctx_rows, ctx_seq=seq_c, lat_seq=seq_l)
        ffn2 = functools.partial(_ffn, x, mod_l, g_l, w_ffn_in_b, w_ffn_out_b, l=l, k=1,
                                 mi=6, gi=4, gid=gid, tm=tm)
        if l < depth - 1:
            x = ffn2()
        else:
            y_prompt = ffn2(row0=0, nrows=ctx_rows).reshape(nb_c, seq_c, d)
            y_sample = ffn2(row0=ctx_rows, nrows=n - ctx_rows).reshape(nb_l, seq_l, d)

    new_k, new_v = (cch.reshape(nb_c, n_even, seq_c, nh, -1) for cch in caches)
    return (y_prompt, y_sample, new_k, new_v,
            jnp.stack(new_sre, axis=1), jnp.stack(new_sim, axis=1))
```
